```python
import jax, jax.numpy as jnp
from jax import lax
import numpy as np

D_MODEL = 1024
BATCH = 32
SEQ = 2048
DEPTH = 1

CHUNK = 64
MIX_WIDTH = D_MODEL
POOL_WIDTH = MIX_WIDTH // 2
POOL_WINDOWS = (2, 4, 8, 16)
N_POOL_GROUPS = len(POOL_WINDOWS)
POOL_GROUP = POOL_WIDTH // N_POOL_GROUPS
SB_WIDTH = MIX_WIDTH - POOL_WIDTH
SB_HEAD_DIM = 64
SB_HEADS = SB_WIDTH // SB_HEAD_DIM
Q_BLOCK = 128
IN_WIDTH = 2 * POOL_WIDTH + 4 * SB_WIDTH
EPS = 1e-6

kernel_name = "hybrid_pool_stickbreak_block"


def rmsnorm(x, g):
    x32 = x.astype(jnp.float32)
    y = x32 * lax.rsqrt(jnp.mean(x32 * x32, axis=-1, keepdims=True) + EPS)
    return y.astype(x.dtype) * g


def pool_mixer(u, w_pool, pool_scale):
    b, s, _ = u.shape
    u32 = u.astype(jnp.float32)
    pos = jnp.arange(s)
    outs = []
    for gi, w in enumerate(POOL_WINDOWS):
        ug = u32[..., gi * POOL_GROUP:(gi + 1) * POOL_GROUP]
        cs = jnp.cumsum(ug, axis=1)
        cs_shift = jnp.concatenate(
            [jnp.zeros((b, w, POOL_GROUP), jnp.float32), cs[:, :-w]], axis=1)
        count = jnp.minimum(pos + 1, w).astype(jnp.float32)[None, :, None]
        pooled = (cs - cs_shift) / count - ug
        outs.append(jnp.einsum('bsc,cd->bsd', pooled.astype(u.dtype), w_pool[gi]))
    return jnp.concatenate(outs, axis=-1) * pool_scale


def stick_breaking_attention(q, k, v):
    s_len = q.shape[2]
    inv_sqrt_d = 1.0 / np.sqrt(SB_HEAD_DIM)
    outs = []
    for i in range(s_len // Q_BLOCK):
        n_keys = (i + 1) * Q_BLOCK
        qb = q[:, :, i * Q_BLOCK:(i + 1) * Q_BLOCK]
        kb = k[:, :, :n_keys]
        vb = v[:, :, :n_keys]
        z = jnp.einsum('bhqd,bhkd->bhqk', qb, kb).astype(jnp.float32) * inv_sqrt_d
        qpos = i * Q_BLOCK + jnp.arange(Q_BLOCK)
        kpos = jnp.arange(n_keys)
        mask = kpos[None, :] < qpos[:, None]
        log_beta = jax.nn.log_sigmoid(z)
        log_1mb = jnp.where(mask, jax.nn.log_sigmoid(-z), 0.0)
        tail = lax.cumsum(log_1mb, axis=3, reverse=True) - log_1mb
        a = jnp.where(mask, jnp.exp(log_beta + tail), 0.0)
        outs.append(jnp.einsum('bhqk,bhkd->bhqd', a.astype(vb.dtype), vb))
    return jnp.concatenate(outs, axis=2)


def setup_inputs(seed: int = 0) -> dict:
    key = jax.random.key(seed)
    ks = jax.random.split(key, 10)
    f32 = jnp.float32
    x = jax.random.normal(ks[0], (BATCH, SEQ, D_MODEL), f32)
    c = jax.random.normal(ks[1], (BATCH, D_MODEL), f32)
    w_ada = jax.random.normal(ks[2], (D_MODEL, 3 * D_MODEL), f32) * (0.1 * D_MODEL ** -0.5)
    b_ada = jax.random.normal(ks[3], (3 * D_MODEL,), f32) * 0.01
    g_pre = 1.0 + 0.02 * jax.random.normal(ks[4], (D_MODEL,), f32)
    w_in = jax.random.normal(ks[5], (D_MODEL, IN_WIDTH), f32) * D_MODEL ** -0.5
    w_pool = jax.random.normal(ks[6], (N_POOL_GROUPS, POOL_GROUP, POOL_GROUP), f32) * POOL_GROUP ** -0.5
    pool_scale = 1.0 + 0.02 * jax.random.normal(ks[7], (POOL_WIDTH,), f32)
    w_out = jax.random.normal(ks[8], (MIX_WIDTH, D_MODEL), f32) * MIX_WIDTH ** -0.5
    g_post = 1.0 + 0.02 * jax.random.normal(ks[9], (D_MODEL,), f32)
    return {"x": x, "c": c, "w_ada": w_ada, "b_ada": b_ada, "g_pre": g_pre,
            "w_in": w_in, "w_pool": w_pool, "pool_scale": pool_scale,
            "w_out": w_out, "g_post": g_post}


def reference(x, c, w_ada, b_ada, g_pre, w_in, w_pool, pool_scale, w_out, g_post):
    b, s, _ = x.shape
    mod = jax.nn.silu(c) @ w_ada + b_ada
    shift, scale, gate = jnp.split(mod, 3, axis=-1)
    for _ in range(DEPTH):
        h = rmsnorm(x, g_pre) * (1.0 + scale[:, None, :]) + shift[:, None, :]
        p = h @ w_in
        u, g_pool, q, k, v, g_sb = jnp.split(
            p, np.cumsum([POOL_WIDTH, POOL_WIDTH, SB_WIDTH, SB_WIDTH, SB_WIDTH]), axis=-1)
        y_pool = pool_mixer(u, w_pool, pool_scale) * jax.nn.silu(g_pool)
        to_heads = lambda t: t.reshape(b, s, SB_HEADS, SB_HEAD_DIM).transpose(0, 2, 1, 3)
        o = stick_breaking_attention(to_heads(q), to_heads(k), to_heads(v))
        y_sb = o.transpose(0, 2, 1, 3).reshape(b, s, SB_WIDTH) * jax.nn.silu(g_sb)
        y = jnp.concatenate([y_pool, y_sb], axis=-1) @ w_out
        x = x + gate[:, None, :] * rmsnorm(y, g_post)
    return x
```

```python
import jax
import jax.numpy as jnp
import numpy as np
from jax import lax
from jax.experimental import pallas as pl
from jax.experimental.pallas import tpu as pltpu

F32 = jnp.float32
BF16 = jnp.bfloat16

EPS = 1e-6
POOL_WINDOWS = (2, 4, 8, 16)
POOL_GROUP = 128
HEAD_DIM = 64
N_HEADS = 8
N_PAIRS = N_HEADS // 2
PAIR_WIDTH = 2 * HEAD_DIM
SB_WIDTH = N_HEADS * HEAD_DIM
POOL_WIDTH = len(POOL_WINDOWS) * POOL_GROUP
MAX_WINDOW = max(POOL_WINDOWS)

TOKEN_TILE = 256
KEY_TILE = 128
ADA_COLS = 512
VMEM_LIMIT_BYTES = 48 * 1024 * 1024

LOG2E = float(np.log2(np.e))
LN2 = float(np.log(2.0))


def _sigmoid(x):
    return 1.0 / (1.0 + jnp.exp(-x))


def _ada_kernel(c_ref, w_ref, b_ref, o_ref):
    c = c_ref[...]
    o_ref[...] = jnp.dot(c * _sigmoid(c), w_ref[...], preferred_element_type=F32,
                         precision=lax.Precision.HIGHEST) + b_ref[...]


def _head_weights(z, carry, ucum, mask):
    k = z.shape[1]
    neg_l1p = jnp.log(1.0 + jnp.exp(-jnp.abs(z))) * -1.0
    log_beta = jnp.minimum(z, 0.0) + neg_l1p
    log_1mb = neg_l1p - jnp.maximum(z, 0.0)
    if mask is not None:
        log_1mb = jnp.where(mask, log_1mb, 0.0)
    hi = log_1mb.astype(BF16)
    lo = (log_1mb - hi.astype(F32)).astype(BF16)
    sums = jnp.dot(jnp.concatenate([hi, lo], axis=1), ucum, preferred_element_type=F32)
    a = jnp.exp(log_beta + (sums[:, :k] + carry))
    if mask is not None:
        a = jnp.where(mask, a, 0.0)
    return a, carry + sums[:, k:]


def _layer_kernel(x_ref, mod_ref, gpre_ref, win_ref, wpool_ref, pscale_ref, wout_ref,
                  gpost_ref, ucum_ref, o_ref, kt_s, v_s, q_s, carry_s, acc_s, uprev_s):
    i = pl.program_id(1)
    tq = x_ref.shape[1]
    nsub = tq // KEY_TILE

    x = x_ref[0]
    mod = mod_ref[0]
    shift, scale, gate = mod[0:1], mod[1:2], mod[2:3]
    hn = x * lax.rsqrt(jnp.mean(x * x, axis=-1, keepdims=True) + EPS) * gpre_ref[...]
    hb = (hn * (1.0 + scale) + shift).astype(BF16)

    def proj(k):
        return jnp.dot(hb, win_ref[:, k * 512:(k + 1) * 512], preferred_element_type=F32)

    q = proj(2) * (1.0 / np.sqrt(HEAD_DIM))
    kt = proj(3).T
    v = proj(4)
    first_dim = lax.broadcasted_iota(jnp.int32, (PAIR_WIDTH, KEY_TILE), 0) < HEAD_DIM
    first_col = lax.broadcasted_iota(jnp.int32, (KEY_TILE, PAIR_WIDTH), 1) < HEAD_DIM
    for p in range(N_PAIRS):
        lanes = slice(p * PAIR_WIDTH, (p + 1) * PAIR_WIDTH)
        q_s[p] = q[:, lanes].astype(BF16)
        for s in range(nsub):
            keys = slice(s * KEY_TILE, (s + 1) * KEY_TILE)
            ktp = kt[lanes, keys]
            kt_s[p, i * nsub + s] = jnp.concatenate(
                [jnp.where(first_dim, ktp, 0.0), jnp.where(first_dim, 0.0, ktp)],
                axis=1).astype(BF16)
            vp = v[keys, lanes]
            v_s[p, i * nsub + s] = jnp.concatenate(
                [jnp.where(first_col, vp, 0.0), jnp.where(first_col, 0.0, vp)],
                axis=0).astype(BF16)

    @pl.when(i == 0)
    def _():
        uprev_s[...] = jnp.zeros_like(uprev_s)

    u = proj(0)
    uext = jnp.concatenate([uprev_s[...], u], axis=0)
    uprev_s[...] = u[tq - MAX_WINDOW:, :]
    pos = i * tq + lax.broadcasted_iota(jnp.int32, (tq, 1), 0)
    pooled_out = []
    for g, w in enumerate(POOL_WINDOWS):
        ug = uext[:, g * POOL_GROUP:(g + 1) * POOL_GROUP]
        acc, width = ug, 1
        while width < w:
            acc = acc + pltpu.roll(acc, width, axis=0)
            width *= 2
        count = jnp.minimum(pos + 1, w).astype(F32)
        pooled = acc[MAX_WINDOW:, :] / count - ug[MAX_WINDOW:, :]
        pooled_out.append(jnp.dot(pooled.astype(BF16), wpool_ref[g],
                                  preferred_element_type=F32))
    g_pool = proj(1)
    y_pool = (jnp.concatenate(pooled_out, axis=-1) * pscale_ref[...]
              * (g_pool * _sigmoid(g_pool)))

    carry_s[...] = jnp.zeros_like(carry_s)
    acc_s[...] = jnp.zeros_like(acc_s)
    key_minus_query = (lax.broadcasted_iota(jnp.int32, (tq, KEY_TILE), 1)
                       - lax.broadcasted_iota(jnp.int32, (tq, KEY_TILE), 0))

    def key_tile_step(j, masked):
        mask = (key_minus_query < i * tq - j * KEY_TILE) if masked else None
        ucum = ucum_ref[...]
        for p in range(N_PAIRS):
            z2 = jnp.dot(q_s[p], kt_s[p, j], preferred_element_type=F32)
            a = []
            for half in range(2):
                h = 2 * p + half
                a_h, carry_s[h] = _head_weights(
                    z2[:, half * KEY_TILE:(half + 1) * KEY_TILE], carry_s[h], ucum, mask)
                a.append(a_h.astype(BF16))
            acc_s[p] += jnp.dot(jnp.concatenate(a, axis=1), v_s[p, j],
                                preferred_element_type=F32)

    def masked_body(n, _):
        key_tile_step((i + 1) * nsub - 1 - n, True)
        return 0

    def plain_body(n, _):
        key_tile_step(i * nsub - 1 - n, False)
        return 0

    lax.fori_loop(0, nsub, masked_body, 0)
    lax.fori_loop(0, i * nsub, plain_body, 0)

    g_sb = proj(5)
    att = jnp.concatenate([acc_s[p] for p in range(N_PAIRS)], axis=-1)
    y_sb = att * (g_sb * _sigmoid(g_sb))

    y = (jnp.dot(y_pool.astype(BF16), wout_ref[:POOL_WIDTH, :], preferred_element_type=F32)
         + jnp.dot(y_sb.astype(BF16), wout_ref[POOL_WIDTH:, :], preferred_element_type=F32))
    yn = y * lax.rsqrt(jnp.mean(y * y, axis=-1, keepdims=True) + EPS) * gpost_ref[...]
    o_ref[0] = x + gate * yn


def _suffix_sum_matrix(k):
    later = np.arange(k)[:, None] > np.arange(k)[None, :]
    half = np.concatenate([later.astype(np.float32), np.ones((k, k), np.float32)], axis=1)
    return jnp.asarray(np.concatenate([half, half], axis=0), dtype=BF16)


@jax.jit
def kernel(x, c, w_ada, b_ada, g_pre, w_in, w_pool, pool_scale, w_out, g_post):
    b, s, d = x.shape
    tq = TOKEN_TILE
    n_key_tiles = s // KEY_TILE
    assert s % tq == 0 and tq % KEY_TILE == 0 and tq >= MAX_WINDOW
    assert w_in.shape == (d, 2 * POOL_WIDTH + 4 * SB_WIDTH) and w_out.shape == (d, d)

    mod = pl.pallas_call(
        _ada_kernel,
        grid=(3 * d // ADA_COLS,),
        in_specs=[pl.BlockSpec((b, d), lambda n: (0, 0)),
                  pl.BlockSpec((d, ADA_COLS), lambda n: (0, n)),
                  pl.BlockSpec((1, ADA_COLS), lambda n: (0, n))],
        out_specs=pl.BlockSpec((b, ADA_COLS), lambda n: (0, n)),
        out_shape=jax.ShapeDtypeStruct((b, 3 * d), F32),
        name="adaln",
    )(c, w_ada, b_ada.reshape(1, 3 * d))
    mod = mod.reshape(b, 3, d)

    const = lambda *shape: pl.BlockSpec(shape, lambda bi, ti: (0,) * len(shape))
    return pl.pallas_call(
        _layer_kernel,
        grid=(b, s // tq),
        in_specs=[pl.BlockSpec((1, tq, d), lambda bi, ti: (bi, ti, 0)),
                  pl.BlockSpec((1, 3, d), lambda bi, ti: (bi, 0, 0)),
                  const(1, d),
                  const(d, w_in.shape[1]),
                  const(len(POOL_WINDOWS), POOL_GROUP, POOL_GROUP),
                  const(1, POOL_WIDTH),
                  const(d, d),
                  const(1, d),
                  const(2 * KEY_TILE, 2 * KEY_TILE)],
        out_specs=pl.BlockSpec((1, tq, d), lambda bi, ti: (bi, ti, 0)),
        out_shape=jax.ShapeDtypeStruct((b, s, d), F32),
        scratch_shapes=[
            pltpu.VMEM((N_PAIRS, n_key_tiles, PAIR_WIDTH, 2 * KEY_TILE), BF16),
            pltpu.VMEM((N_PAIRS, n_key_tiles, 2 * KEY_TILE, PAIR_WIDTH), BF16),
            pltpu.VMEM((N_PAIRS, tq, PAIR_WIDTH), BF16),
            pltpu.VMEM((N_HEADS, tq, KEY_TILE), F32),
            pltpu.VMEM((N_PAIRS, tq, PAIR_WIDTH), F32),
            pltpu.VMEM((MAX_WINDOW, POOL_WIDTH), F32)],
        compiler_params=pltpu.CompilerParams(
            dimension_semantics=("arbitrary", "arbitrary"),
            vmem_limit_bytes=VMEM_LIMIT_BYTES),
        name="layer",
    )(x, mod, g_pre.reshape(1, d), w_in.astype(BF16), w_pool.astype(BF16),
      pool_scale.reshape(1, POOL_WIDTH), w_out.astype(BF16), g_post.reshape(1, d),
      _suffix_sum_matrix(KEY_TILE))
```

```python
import jax
import jax.numpy as jnp
import numpy as np
from jax import lax
from jax.experimental import pallas as pl
from jax.experimental.pallas import tpu as pltpu

F32 = jnp.float32
BF16 = jnp.bfloat16

EPS = 1e-6
POOL_WINDOWS = (2, 4, 8, 16)
POOL_GROUP = 128
HEAD_DIM = 64
N_HEADS = 8
N_PAIRS = N_HEADS // 2
PAIR_WIDTH = 2 * HEAD_DIM
SB_WIDTH = N_HEADS * HEAD_DIM
POOL_WIDTH = len(POOL_WINDOWS) * POOL_GROUP
MAX_WINDOW = max(POOL_WINDOWS)

TOKEN_TILE = 256
ATT_TILE = 128
ADA_COLS = 512
VMEM_LIMIT_BYTES = 48 * 1024 * 1024

SKIP_BELOW = -89.0


def _sigmoid(x):
    return 1.0 / (1.0 + jnp.exp(-x))


def _ada_kernel(c_ref, w_ref, b_ref, o_ref):
    c = c_ref[...]
    o_ref[...] = jnp.dot(c * _sigmoid(c), w_ref[...], preferred_element_type=F32,
                         precision=lax.Precision.HIGHEST) + b_ref[...]


def _log_terms(z, mask):
    neg_part = jnp.minimum(z, 0.0)
    neg_pos_part = neg_part - z
    log1p_term = jnp.log(1.0 + jnp.exp(neg_part + neg_pos_part))
    log_beta = neg_part - log1p_term
    log_1mb = neg_pos_part - log1p_term
    if mask is not None:
        log_1mb = jnp.where(mask, log_1mb, 0.0)
    return log_beta, log_1mb


def _hi_lo(x):
    hi = x.astype(BF16)
    lo = (x - hi.astype(F32)).astype(BF16)
    return jnp.concatenate([hi, lo], axis=0)


def _layer_kernel(x_ref, mod_ref, gpre_ref, win_ref, wpool_ref, pscale_ref, wout_ref,
                  gpost_ref, ucum_ref, o_ref, k_s, vt_s, qbd_s, acc_s, uprev_s):
    i = pl.program_id(1)
    tq = x_ref.shape[1]
    nsub = tq // ATT_TILE

    x = x_ref[0]
    mod = mod_ref[0]
    shift, scale, gate = mod[0:1], mod[1:2], mod[2:3]
    hn = x * lax.rsqrt(jnp.mean(x * x, axis=-1, keepdims=True) + EPS) * gpre_ref[...]
    hb = (hn * (1.0 + scale) + shift).astype(BF16)

    def proj(k):
        return jnp.dot(hb, win_ref[:, k * 512:(k + 1) * 512], preferred_element_type=F32)

    @pl.when(i == 0)
    def _():
        uprev_s[...] = jnp.zeros_like(uprev_s)
        k_s[0] = jnp.zeros_like(k_s[0])
        vt_s[0] = jnp.zeros_like(vt_s[0])

    qt = (proj(2) * (1.0 / np.sqrt(HEAD_DIM))).T.astype(BF16)
    k = proj(3).astype(BF16)
    vt = proj(4).T.astype(BF16)
    first_head = lax.broadcasted_iota(jnp.int32, (PAIR_WIDTH, ATT_TILE), 0) < HEAD_DIM
    zero = jnp.zeros((PAIR_WIDTH, ATT_TILE), BF16)
    for s in range(nsub):
        cols = slice(s * ATT_TILE, (s + 1) * ATT_TILE)
        k_s[i * nsub + s + 1] = k[cols, :]
        vt_s[i * nsub + s + 1] = vt[:, cols]
        for p in range(N_PAIRS):
            qp = qt[p * PAIR_WIDTH:(p + 1) * PAIR_WIDTH, cols]
            qbd_s[p, s] = jnp.concatenate(
                [jnp.where(first_head, qp, zero), jnp.where(first_head, zero, qp)], axis=1)

    u = proj(0)
    uext = jnp.concatenate([uprev_s[...], u], axis=0)
    uprev_s[...] = u[tq - MAX_WINDOW:, :]
    pos = i * tq + lax.broadcasted_iota(jnp.int32, (tq, 1), 0)
    pooled_out = []
    for g, w in enumerate(POOL_WINDOWS):
        ug = uext[:, g * POOL_GROUP:(g + 1) * POOL_GROUP]
        acc, width = ug, 1
        while width < w:
            acc = acc + pltpu.roll(acc, width, axis=0)
            width *= 2
        count = jnp.minimum(pos + 1, w).astype(F32)
        pooled = acc[MAX_WINDOW:, :] / count - ug[MAX_WINDOW:, :]
        pooled_out.append(jnp.dot(pooled.astype(BF16), wpool_ref[g],
                                  preferred_element_type=F32))
    g_pool = proj(1)
    y_pool = (jnp.concatenate(pooled_out, axis=-1) * pscale_ref[...]
              * (g_pool * _sigmoid(g_pool)))

    ucum_t = ucum_ref[...]
    key_idx = lax.broadcasted_iota(jnp.int32, (ATT_TILE, 2 * ATT_TILE), 0)
    query_idx = lax.broadcasted_iota(jnp.int32, (ATT_TILE, 2 * ATT_TILE), 1) % ATT_TILE
    causal = key_idx < query_idx

    def attention_step(n, carries, mask):
        chains = [(s, p) for s in range(nsub) for p in range(N_PAIRS)]
        slots = [jnp.maximum(i * nsub + s - n + 1, 0) for s in range(nsub)]
        lanes = [slice(p * PAIR_WIDTH, (p + 1) * PAIR_WIDTH) for p in range(N_PAIRS)]
        z = [jnp.dot(k_s[slots[s], :, lanes[p]], qbd_s[p, s], preferred_element_type=F32)
             for s, p in chains]
        logs = [_log_terms(zc, mask) for zc in z]
        tail = [jnp.dot(ucum_t, _hi_lo(log_1mb), preferred_element_type=F32)
                for _, log_1mb in logs]
        new_carries = []
        for c, (s, p) in enumerate(chains):
            log_beta, log_1mb = logs[c]
            a = jnp.exp(log_beta + (tail[c] + carries[c]))
            if mask is not None:
                a = jnp.where(mask, a, 0.0)
            o = jnp.dot(vt_s[slots[s], lanes[p], :], a.astype(BF16),
                        preferred_element_type=F32)
            o = jnp.concatenate([o[:HEAD_DIM, :ATT_TILE], o[HEAD_DIM:, ATT_TILE:]], axis=0)
            if mask is not None:
                acc_s[p, s] = o
            else:
                acc_s[p, s] += o
            new_carries.append(carries[c] + (tail[c][0:1, :] + log_1mb[0:1, :]))
        return tuple(new_carries)

    def needed(n, carries):
        go = None
        for s in range(nsub):
            live = carries[s * N_PAIRS]
            for p in range(1, N_PAIRS):
                live = jnp.maximum(live, carries[s * N_PAIRS + p])
            go_s = jnp.logical_and(i * nsub + s - n >= 0, jnp.max(live) >= SKIP_BELOW)
            go = go_s if go is None else jnp.logical_or(go, go_s)
        return go

    carries = attention_step(0, (jnp.zeros((1, 2 * ATT_TILE), F32),) * (nsub * N_PAIRS), causal)

    def loop_body(state):
        n, _, carries = state
        carries = attention_step(n, carries, None)
        return n + 1, needed(n + 1, carries), carries

    lax.while_loop(lambda state: state[1], loop_body, (jnp.int32(1), needed(1, carries), carries))

    att_t = jnp.concatenate(
        [jnp.concatenate([acc_s[p, s] for s in range(nsub)], axis=1)
         for p in range(N_PAIRS)], axis=0)
    g_sb = proj(5)
    y_sb = att_t.T * (g_sb * _sigmoid(g_sb))

    y = (jnp.dot(y_pool.astype(BF16), wout_ref[:POOL_WIDTH, :], preferred_element_type=F32)
         + jnp.dot(y_sb.astype(BF16), wout_ref[POOL_WIDTH:, :], preferred_element_type=F32))
    yn = y * lax.rsqrt(jnp.mean(y * y, axis=-1, keepdims=True) + EPS) * gpost_ref[...]
    o_ref[0] = x + gate * yn


def _suffix_sum_matrix_t(k):
    later = np.arange(k)[None, :] > np.arange(k)[:, None]
    return jnp.asarray(np.concatenate([later, later], axis=1).astype(np.float32), dtype=BF16)


@jax.jit
def kernel(x, c, w_ada, b_ada, g_pre, w_in, w_pool, pool_scale, w_out, g_post):
    b, s, d = x.shape
    tq = TOKEN_TILE
    n_key_tiles = s // ATT_TILE
    assert s % tq == 0 and tq % ATT_TILE == 0 and tq >= MAX_WINDOW
    assert w_in.shape == (d, 2 * POOL_WIDTH + 4 * SB_WIDTH) and w_out.shape == (d, d)

    mod = pl.pallas_call(
        _ada_kernel,
        grid=(3 * d // ADA_COLS,),
        in_specs=[pl.BlockSpec((b, d), lambda n: (0, 0)),
                  pl.BlockSpec((d, ADA_COLS), lambda n: (0, n)),
                  pl.BlockSpec((1, ADA_COLS), lambda n: (0, n))],
        out_specs=pl.BlockSpec((b, ADA_COLS), lambda n: (0, n)),
        out_shape=jax.ShapeDtypeStruct((b, 3 * d), F32),
        name="adaln",
    )(c, w_ada, b_ada.reshape(1, 3 * d))
    mod = mod.reshape(b, 3, d)

    const = lambda *shape: pl.BlockSpec(shape, lambda bi, ti: (0,) * len(shape))
    return pl.pallas_call(
        _layer_kernel,
        grid=(b, s // tq),
        in_specs=[pl.BlockSpec((1, tq, d), lambda bi, ti: (bi, ti, 0)),
                  pl.BlockSpec((1, 3, d), lambda bi, ti: (bi, 0, 0)),
                  const(1, d),
                  const(d, w_in.shape[1]),
                  const(len(POOL_WINDOWS), POOL_GROUP, POOL_GROUP),
                  const(1, POOL_WIDTH),
                  const(d, d),
                  const(1, d),
                  const(ATT_TILE, 2 * ATT_TILE)],
        out_specs=pl.BlockSpec((1, tq, d), lambda bi, ti: (bi, ti, 0)),
        out_shape=jax.ShapeDtypeStruct((b, s, d), F32),
        scratch_shapes=[
            pltpu.VMEM((n_key_tiles + 1, ATT_TILE, SB_WIDTH), BF16),
            pltpu.VMEM((n_key_tiles + 1, SB_WIDTH, ATT_TILE), BF16),
            pltpu.VMEM((N_PAIRS, tq // ATT_TILE, PAIR_WIDTH, 2 * ATT_TILE), BF16),
            pltpu.VMEM((N_PAIRS, tq // ATT_TILE, PAIR_WIDTH, ATT_TILE), F32),
            pltpu.VMEM((MAX_WINDOW, POOL_WIDTH), F32)],
        compiler_params=pltpu.CompilerParams(
            dimension_semantics=("arbitrary", "arbitrary"),
            vmem_limit_bytes=VMEM_LIMIT_BYTES),
        name="layer",
    )(x, mod, g_pre.reshape(1, d), w_in.astype(BF16), w_pool.astype(BF16),
      pool_scale.reshape(1, POOL_WIDTH), w_out.astype(BF16), g_post.reshape(1, d),
      _suffix_sum_matrix_t(ATT_TILE))
```

```python
import jax
import jax.numpy as jnp
import numpy as np
from jax import lax
from jax.experimental import pallas as pl
from jax.experimental.pallas import tpu as pltpu

F32 = jnp.float32
BF16 = jnp.bfloat16

EPS = 1e-6
POOL_WINDOWS = (2, 4, 8, 16)
POOL_GROUP = 128
HEAD_DIM = 64
N_HEADS = 8
N_PAIRS = N_HEADS // 2
PAIR_WIDTH = 2 * HEAD_DIM
SB_WIDTH = N_HEADS * HEAD_DIM
POOL_WIDTH = len(POOL_WINDOWS) * POOL_GROUP
MAX_WINDOW = max(POOL_WINDOWS)

TOKEN_TILE = 256
ATT_TILE = 128
MXU_COLS = 256
ADA_COLS = 512
VMEM_LIMIT_BYTES = 48 * 1024 * 1024

SKIP_BELOW = -89.0


def _sigmoid(x):
    return 1.0 / (1.0 + jnp.exp(-x))


def _ada_kernel(c_ref, w_ref, b_ref, o_ref):
    c = c_ref[...]
    o_ref[...] = jnp.dot(c * _sigmoid(c), w_ref[...], preferred_element_type=F32,
                         precision=lax.Precision.HIGHEST) + b_ref[...]


def _log_terms(z, mask):
    neg_part = jnp.minimum(z, 0.0)
    neg_pos_part = neg_part - z
    log1p_term = jnp.log(1.0 + jnp.exp(neg_part + neg_pos_part))
    log_beta = neg_part - log1p_term
    log_1mb = neg_pos_part - log1p_term
    if mask is not None:
        log_1mb = jnp.where(mask, log_1mb, 0.0)
    return log_beta, log_1mb


def _hi_lo(x):
    hi = x.astype(BF16)
    lo = (x - hi.astype(F32)).astype(BF16)
    return jnp.concatenate([hi, lo], axis=0)


def _layer_kernel(x_ref, mod_ref, gpre_ref, win_ref, wpool_ref, pscale_ref, wout_ref,
                  gpost_ref, ucum_ref, o_ref, k_s, vt_s, qbd_s, acc_s, uprev_s, y_s, gate_s):
    i = pl.program_id(1)
    tq = x_ref.shape[1]
    nsub = tq // ATT_TILE

    @pl.when(i == 0)
    def _():
        uprev_s[...] = jnp.zeros_like(uprev_s)
        k_s[0] = jnp.zeros_like(k_s[0])
        vt_s[0] = jnp.zeros_like(vt_s[0])

    x = x_ref[0]
    mod = mod_ref[0]
    shift, scale, gate = mod[0:1], mod[1:2], mod[2:3]
    hn = x * lax.rsqrt(jnp.mean(x * x, axis=-1, keepdims=True) + EPS) * gpre_ref[...]
    hb = (hn * (1.0 + scale) + shift).astype(BF16)

    def proj(first_col, width):
        return jnp.dot(hb, win_ref[:, first_col:first_col + width], preferred_element_type=F32)


    qt = (proj(2 * 512, 512) * (1.0 / np.sqrt(HEAD_DIM))).T.astype(BF16)
    k = proj(3 * 512, 512).astype(BF16)
    first_head = lax.broadcasted_iota(jnp.int32, (PAIR_WIDTH, ATT_TILE), 0) < HEAD_DIM
    zero = jnp.zeros((PAIR_WIDTH, ATT_TILE), BF16)
    for s in range(nsub):
        cols = slice(s * ATT_TILE, (s + 1) * ATT_TILE)
        k_s[i * nsub + s + 1] = k[cols, :]
        for p in range(N_PAIRS):
            qp = qt[p * PAIR_WIDTH:(p + 1) * PAIR_WIDTH, cols]
            qbd_s[p, s] = jnp.concatenate(
                [jnp.where(first_head, qp, zero), jnp.where(first_head, zero, qp)], axis=1)

    def values_chunk(half):
        def emit():
            rows = slice(half * MXU_COLS, (half + 1) * MXU_COLS)
            vt = proj(4 * 512 + half * MXU_COLS, MXU_COLS).T.astype(BF16)
            for s in range(nsub):
                vt_s[i * nsub + s + 1, rows, :] = vt[:, s * ATT_TILE:(s + 1) * ATT_TILE]
        return emit

    pooled_out = [None] * len(POOL_WINDOWS)
    u_half = [None] * (POOL_WIDTH // MXU_COLS)

    def pool_in_chunk(half):
        def emit():
            u_half[half] = proj(half * MXU_COLS, MXU_COLS)
        return emit

    def pool_chunk(half):
        def emit():
            cols = slice(half * MXU_COLS, (half + 1) * MXU_COLS)
            u = u_half[half]
            uext = jnp.concatenate([uprev_s[:, cols], u], axis=0)
            uprev_s[:, cols] = u[tq - MAX_WINDOW:, :]
            pos = i * tq + lax.broadcasted_iota(jnp.int32, (tq, 1), 0)
            for gg in range(MXU_COLS // POOL_GROUP):
                g = half * (MXU_COLS // POOL_GROUP) + gg
                w = POOL_WINDOWS[g]
                ug = uext[:, gg * POOL_GROUP:(gg + 1) * POOL_GROUP]
                acc, width = ug, 1
                while width < w:
                    acc = acc + pltpu.roll(acc, width, axis=0)
                    width *= 2
                count = jnp.minimum(pos + 1, w).astype(F32)
                pooled = acc[MAX_WINDOW:, :] / count - ug[MAX_WINDOW:, :]
                pooled_out[g] = jnp.dot(pooled.astype(BF16), wpool_ref[g],
                                        preferred_element_type=F32)
        return emit

    def gate_chunk(slab, half):
        def emit():
            cols = slice(half * MXU_COLS, (half + 1) * MXU_COLS)
            g = proj(slab * 512 + half * MXU_COLS, MXU_COLS)
            gate_s[0 if slab == 1 else 1, :, cols] = g * _sigmoid(g)
        return emit

    gated = []

    def pool_out_chunk(j):
        def emit():
            if not gated:
                y_pool = jnp.concatenate(pooled_out, axis=-1) * pscale_ref[...]
                gated.append((y_pool * gate_s[0]).astype(BF16))
            cols = slice(j * MXU_COLS, (j + 1) * MXU_COLS)
            y_s[:, cols] = jnp.dot(gated[0], wout_ref[:POOL_WIDTH, cols],
                                   preferred_element_type=F32)
        return emit

    ucum_t = ucum_ref[...]
    key_idx = lax.broadcasted_iota(jnp.int32, (ATT_TILE, 2 * ATT_TILE), 0)
    query_idx = lax.broadcasted_iota(jnp.int32, (ATT_TILE, 2 * ATT_TILE), 1) % ATT_TILE
    causal = key_idx < query_idx
    chains = [(s, p) for s in range(nsub) for p in range(N_PAIRS)]
    lanes = [slice(p * PAIR_WIDTH, (p + 1) * PAIR_WIDTH) for p in range(N_PAIRS)]

    def attention_step(n, carries, mask, after_sums=(), after_values=()):
        slots = [jnp.maximum(i * nsub + s - n + 1, 0) for s in range(nsub)]
        z = [jnp.dot(k_s[slots[s], :, lanes[p]], qbd_s[p, s], preferred_element_type=F32)
             for s, p in chains]
        logs, tail = [], []
        for c in range(len(chains)):
            logs.append(_log_terms(z[c], mask))
            tail.append(jnp.dot(ucum_t, _hi_lo(logs[c][1]), preferred_element_type=F32))
            if c < len(after_sums):
                after_sums[c]()
        new_carries = []
        for c, (s, p) in enumerate(chains):
            log_beta, log_1mb = logs[c]
            a = jnp.exp(log_beta + (tail[c] + carries[c]))
            if mask is not None:
                a = jnp.where(mask, a, 0.0)
            o = jnp.dot(vt_s[slots[s], lanes[p], :], a.astype(BF16),
                        preferred_element_type=F32)
            o = jnp.concatenate([o[:HEAD_DIM, :ATT_TILE], o[HEAD_DIM:, ATT_TILE:]], axis=0)
            if mask is not None:
                acc_s[p, s] = o
            else:
                acc_s[p, s] += o
            new_carries.append(carries[c] + (tail[c][0:1, :] + log_1mb[0:1, :]))
            if c < len(after_values):
                after_values[c]()
        return tuple(new_carries)

    def needed(n, carries):
        go = None
        for s in range(nsub):
            live = carries[s * N_PAIRS]
            for p in range(1, N_PAIRS):
                live = jnp.maximum(live, carries[s * N_PAIRS + p])
            go_s = jnp.logical_and(i * nsub + s - n >= 0, jnp.max(live) >= SKIP_BELOW)
            go = go_s if go is None else jnp.logical_or(go, go_s)
        return go

    skip = lambda: None
    carries = attention_step(
        0, (jnp.zeros((1, 2 * ATT_TILE), F32),) * len(chains), causal,
        after_sums=[values_chunk(0), skip, values_chunk(1), skip,
                    pool_in_chunk(0), skip, pool_in_chunk(1), skip],
        after_values=[skip, gate_chunk(1, 0), skip, gate_chunk(1, 1)])
    carries = attention_step(
        1, carries, None,
        after_sums=[gate_chunk(5, 0), skip, gate_chunk(5, 1), skip,
                    pool_chunk(0), skip, pool_chunk(1), skip],
        after_values=[pool_out_chunk(j // 2) if j % 2 == 0 else skip
                      for j in range(2 * wout_ref.shape[1] // MXU_COLS)])

    def loop_body(state):
        n, _, carries = state
        carries = attention_step(n, carries, None)
        return n + 1, needed(n + 1, carries), carries

    lax.while_loop(lambda state: state[1], loop_body, (jnp.int32(2), needed(2, carries), carries))

    att_t = jnp.concatenate(
        [jnp.concatenate([acc_s[p, s] for s in range(nsub)], axis=1)
         for p in range(N_PAIRS)], axis=0)
    y_sb = att_t.T * gate_s[1]

    y = y_s[...] + jnp.dot(y_sb.astype(BF16), wout_ref[POOL_WIDTH:, :],
                           preferred_element_type=F32)
    yn = y * lax.rsqrt(jnp.mean(y * y, axis=-1, keepdims=True) + EPS) * gpost_ref[...]
    o_ref[0] = x + gate * yn


def _suffix_sum_matrix_t(k):
    later = np.arange(k)[None, :] > np.arange(k)[:, None]
    return jnp.asarray(np.concatenate([later, later], axis=1).astype(np.float32), dtype=BF16)


@jax.jit
def kernel(x, c, w_ada, b_ada, g_pre, w_in, w_pool, pool_scale, w_out, g_post):
    b, s, d = x.shape
    tq = TOKEN_TILE
    nsub = tq // ATT_TILE
    n_key_tiles = s // ATT_TILE
    assert s % tq == 0 and tq % ATT_TILE == 0 and tq >= MAX_WINDOW
    assert w_in.shape == (d, 2 * POOL_WIDTH + 4 * SB_WIDTH) and w_out.shape == (d, d)

    mod = pl.pallas_call(
        _ada_kernel,
        grid=(3 * d // ADA_COLS,),
        in_specs=[pl.BlockSpec((b, d), lambda n: (0, 0)),
                  pl.BlockSpec((d, ADA_COLS), lambda n: (0, n)),
                  pl.BlockSpec((1, ADA_COLS), lambda n: (0, n))],
        out_specs=pl.BlockSpec((b, ADA_COLS), lambda n: (0, n)),
        out_shape=jax.ShapeDtypeStruct((b, 3 * d), F32),
        name="adaln",
    )(c, w_ada, b_ada.reshape(1, 3 * d))
    mod = mod.reshape(b, 3, d)

    const = lambda *shape: pl.BlockSpec(shape, lambda bi, ti: (0,) * len(shape))
    return pl.pallas_call(
        _layer_kernel,
        grid=(b, s // tq),
        in_specs=[pl.BlockSpec((1, tq, d), lambda bi, ti: (bi, ti, 0)),
                  pl.BlockSpec((1, 3, d), lambda bi, ti: (bi, 0, 0)),
                  const(1, d),
                  const(d, w_in.shape[1]),
                  const(len(POOL_WINDOWS), POOL_GROUP, POOL_GROUP),
                  const(1, POOL_WIDTH),
                  const(d, d),
                  const(1, d),
                  const(ATT_TILE, 2 * ATT_TILE)],
        out_specs=pl.BlockSpec((1, tq, d), lambda bi, ti: (bi, ti, 0)),
        out_shape=jax.ShapeDtypeStruct((b, s, d), F32),
        scratch_shapes=[
            pltpu.VMEM((n_key_tiles + 1, ATT_TILE, SB_WIDTH), BF16),
            pltpu.VMEM((n_key_tiles + 1, SB_WIDTH, ATT_TILE), BF16),
            pltpu.VMEM((N_PAIRS, nsub, PAIR_WIDTH, 2 * ATT_TILE), BF16),
            pltpu.VMEM((N_PAIRS, nsub, PAIR_WIDTH, ATT_TILE), F32),
            pltpu.VMEM((MAX_WINDOW, POOL_WIDTH), F32),
            pltpu.VMEM((tq, d), F32),
            pltpu.VMEM((2, tq, POOL_WIDTH), F32)],
        compiler_params=pltpu.CompilerParams(
            dimension_semantics=("arbitrary", "arbitrary"),
            vmem_limit_bytes=VMEM_LIMIT_BYTES),
        name="layer",
    )(x, mod, g_pre.reshape(1, d), w_in.astype(BF16), w_pool.astype(BF16),
      pool_scale.reshape(1, POOL_WIDTH), w_out.astype(BF16), g_post.reshape(1, d),
      _suffix_sum_matrix_t(ATT_TILE))
```

```python
import jax
import jax.numpy as jnp
import numpy as np
from jax import lax
from jax.experimental import pallas as pl
from jax.experimental.pallas import tpu as pltpu

F32 = jnp.float32
BF16 = jnp.bfloat16

EPS = 1e-6
POOL_WINDOWS = (2, 4, 8, 16)
POOL_GROUP = 128
HEAD_DIM = 64
N_HEADS = 8
N_PAIRS = N_HEADS // 2
PAIR_WIDTH = 2 * HEAD_DIM
SB_WIDTH = N_HEADS * HEAD_DIM
POOL_WIDTH = len(POOL_WINDOWS) * POOL_GROUP
MAX_WINDOW = max(POOL_WINDOWS)

TOKEN_TILE = 256
TILES_PER_STEP = 2
ATT_TILE = 128
MXU_COLS = 256
ADA_COLS = 512
VMEM_LIMIT_BYTES = 48 * 1024 * 1024

SKIP_BELOW = -89.0


def _sigmoid(x):
    return 1.0 / (1.0 + jnp.exp(-x))


def _ada_kernel(c_ref, w_ref, b_ref, o_ref):
    c = c_ref[...]
    o_ref[...] = jnp.dot(c * _sigmoid(c), w_ref[...], preferred_element_type=F32,
                         precision=lax.Precision.HIGHEST) + b_ref[...]


def _log_terms(z, mask):
    neg_part = jnp.minimum(z, 0.0)
    neg_pos_part = neg_part - z
    log1p_term = jnp.log(1.0 + jnp.exp(neg_part + neg_pos_part))
    log_beta = neg_part - log1p_term
    log_1mb = neg_pos_part - log1p_term
    if mask is not None:
        log_1mb = jnp.where(mask, log_1mb, 0.0)
    return log_beta, log_1mb


def _hi_lo(x):
    hi = x.astype(BF16)
    lo = (x - hi.astype(F32)).astype(BF16)
    return jnp.concatenate([hi, lo], axis=0)


def _token_tile(i, rows, x_ref, mod_ref, gpre_ref, win_ref, wpool_ref, pscale_ref, wout_ref,
                gpost_ref, ucum_ref, o_ref, k_s, vt_s, qbd_s, acc_s, uprev_s, y_s, gate_s):
    tq = rows.stop - rows.start
    nsub = tq // ATT_TILE

    x = x_ref[0, rows, :]
    mod = mod_ref[0]
    shift, scale, gate = mod[0:1], mod[1:2], mod[2:3]
    hn = x * lax.rsqrt(jnp.mean(x * x, axis=-1, keepdims=True) + EPS) * gpre_ref[...]
    hb = (hn * (1.0 + scale) + shift).astype(BF16)

    def proj(first_col, width):
        return jnp.dot(hb, win_ref[:, first_col:first_col + width], preferred_element_type=F32)


    qt = (proj(2 * 512, 512) * (1.0 / np.sqrt(HEAD_DIM))).T.astype(BF16)
    k = proj(3 * 512, 512).astype(BF16)
    first_head = lax.broadcasted_iota(jnp.int32, (PAIR_WIDTH, ATT_TILE), 0) < HEAD_DIM
    zero = jnp.zeros((PAIR_WIDTH, ATT_TILE), BF16)
    for s in range(nsub):
        cols = slice(s * ATT_TILE, (s + 1) * ATT_TILE)
        k_s[i * nsub + s + 1] = k[cols, :]
        for p in range(N_PAIRS):
            qp = qt[p * PAIR_WIDTH:(p + 1) * PAIR_WIDTH, cols]
            qbd_s[p, s] = jnp.concatenate(
                [jnp.where(first_head, qp, zero), jnp.where(first_head, zero, qp)], axis=1)

    def values_chunk(half):
        def emit():
            rows = slice(half * MXU_COLS, (half + 1) * MXU_COLS)
            vt = proj(4 * 512 + half * MXU_COLS, MXU_COLS).T.astype(BF16)
            for s in range(nsub):
                vt_s[i * nsub + s + 1, rows, :] = vt[:, s * ATT_TILE:(s + 1) * ATT_TILE]
        return emit

    pooled_out = [None] * len(POOL_WINDOWS)
    u_half = [None] * (POOL_WIDTH // MXU_COLS)

    def pool_in_chunk(half):
        def emit():
            u_half[half] = proj(half * MXU_COLS, MXU_COLS)
        return emit

    def pool_chunk(half):
        def emit():
            cols = slice(half * MXU_COLS, (half + 1) * MXU_COLS)
            u = u_half[half]
            uext = jnp.concatenate([uprev_s[:, cols], u], axis=0)
            uprev_s[:, cols] = u[tq - MAX_WINDOW:, :]
            pos = i * tq + lax.broadcasted_iota(jnp.int32, (tq, 1), 0)
            for gg in range(MXU_COLS // POOL_GROUP):
                g = half * (MXU_COLS // POOL_GROUP) + gg
                w = POOL_WINDOWS[g]
                ug = uext[:, gg * POOL_GROUP:(gg + 1) * POOL_GROUP]
                acc, width = ug, 1
                while width < w:
                    acc = acc + pltpu.roll(acc, width, axis=0)
                    width *= 2
                count = jnp.minimum(pos + 1, w).astype(F32)
                pooled = acc[MAX_WINDOW:, :] / count - ug[MAX_WINDOW:, :]
                pooled_out[g] = jnp.dot(pooled.astype(BF16), wpool_ref[g],
                                        preferred_element_type=F32)
        return emit

    def gate_chunk(slab, half):
        def emit():
            cols = slice(half * MXU_COLS, (half + 1) * MXU_COLS)
            g = proj(slab * 512 + half * MXU_COLS, MXU_COLS)
            gate_s[0 if slab == 1 else 1, :, cols] = g * _sigmoid(g)
        return emit

    gated = []

    def pool_out_chunk(j):
        def emit():
            if not gated:
                y_pool = jnp.concatenate(pooled_out, axis=-1) * pscale_ref[...]
                gated.append((y_pool * gate_s[0]).astype(BF16))
            cols = slice(j * MXU_COLS, (j + 1) * MXU_COLS)
            y_s[:, cols] = jnp.dot(gated[0], wout_ref[:POOL_WIDTH, cols],
                                   preferred_element_type=F32)
        return emit

    ucum_t = ucum_ref[...]
    key_idx = lax.broadcasted_iota(jnp.int32, (ATT_TILE, 2 * ATT_TILE), 0)
    query_idx = lax.broadcasted_iota(jnp.int32, (ATT_TILE, 2 * ATT_TILE), 1) % ATT_TILE
    causal = key_idx < query_idx
    chains = [(s, p) for s in range(nsub) for p in range(N_PAIRS)]
    lanes = [slice(p * PAIR_WIDTH, (p + 1) * PAIR_WIDTH) for p in range(N_PAIRS)]

    def attention_step(n, carries, mask, after_sums=(), after_values=()):
        slots = [jnp.maximum(i * nsub + s - n + 1, 0) for s in range(nsub)]
        z = [jnp.dot(k_s[slots[s], :, lanes[p]], qbd_s[p, s], preferred_element_type=F32)
             for s, p in chains]
        logs, tail = [], []
        for c in range(len(chains)):
            logs.append(_log_terms(z[c], mask))
            tail.append(jnp.dot(ucum_t, _hi_lo(logs[c][1]), preferred_element_type=F32))
            if c < len(after_sums):
                after_sums[c]()
        new_carries = []
        for c, (s, p) in enumerate(chains):
            log_beta, log_1mb = logs[c]
            a = jnp.exp(log_beta + (tail[c] + carries[c]))
            if mask is not None:
                a = jnp.where(mask, a, 0.0)
            o = jnp.dot(vt_s[slots[s], lanes[p], :], a.astype(BF16),
                        preferred_element_type=F32)
            o = jnp.concatenate([o[:HEAD_DIM, :ATT_TILE], o[HEAD_DIM:, ATT_TILE:]], axis=0)
            if mask is not None:
                acc_s[p, s] = o
            else:
                acc_s[p, s] += o
            new_carries.append(carries[c] + (tail[c][0:1, :] + log_1mb[0:1, :]))
            if c < len(after_values):
                after_values[c]()
        return tuple(new_carries)

    def needed(n, carries):
        go = None
        for s in range(nsub):
            live = carries[s * N_PAIRS]
            for p in range(1, N_PAIRS):
                live = jnp.maximum(live, carries[s * N_PAIRS + p])
            go_s = jnp.logical_and(i * nsub + s - n >= 0, jnp.max(live) >= SKIP_BELOW)
            go = go_s if go is None else jnp.logical_or(go, go_s)
        return go

    skip = lambda: None
    carries = attention_step(
        0, (jnp.zeros((1, 2 * ATT_TILE), F32),) * len(chains), causal,
        after_sums=[values_chunk(0), skip, values_chunk(1), skip,
                    pool_in_chunk(0), skip, pool_in_chunk(1), skip],
        after_values=[skip, gate_chunk(1, 0), skip, gate_chunk(1, 1)])
    carries = attention_step(
        1, carries, None,
        after_sums=[gate_chunk(5, 0), skip, gate_chunk(5, 1), skip,
                    pool_chunk(0), skip, pool_chunk(1), skip],
        after_values=[pool_out_chunk(j // 2) if j % 2 == 0 else skip
                      for j in range(2 * wout_ref.shape[1] // MXU_COLS)])

    def loop_body(state):
        n, _, carries = state
        carries = attention_step(n, carries, None)
        return n + 1, needed(n + 1, carries), carries

    lax.while_loop(lambda state: state[1], loop_body, (jnp.int32(2), needed(2, carries), carries))

    att_t = jnp.concatenate(
        [jnp.concatenate([acc_s[p, s] for s in range(nsub)], axis=1)
         for p in range(N_PAIRS)], axis=0)
    y_sb = att_t.T * gate_s[1]

    y = y_s[...] + jnp.dot(y_sb.astype(BF16), wout_ref[POOL_WIDTH:, :],
                           preferred_element_type=F32)
    yn = y * lax.rsqrt(jnp.mean(y * y, axis=-1, keepdims=True) + EPS) * gpost_ref[...]
    o_ref[0, rows, :] = x + gate * yn


def _layer_kernel(x_ref, mod_ref, gpre_ref, win_ref, wpool_ref, pscale_ref, wout_ref,
                  gpost_ref, ucum_ref, o_ref, k_s, vt_s, qbd_s, acc_s, uprev_s, y_s, gate_s):
    step = pl.program_id(1)

    @pl.when(step == 0)
    def _():
        uprev_s[...] = jnp.zeros_like(uprev_s)
        k_s[0] = jnp.zeros_like(k_s[0])
        vt_s[0] = jnp.zeros_like(vt_s[0])

    for t in range(TILES_PER_STEP):
        _token_tile(step * TILES_PER_STEP + t, slice(t * TOKEN_TILE, (t + 1) * TOKEN_TILE),
                    x_ref, mod_ref, gpre_ref, win_ref, wpool_ref, pscale_ref, wout_ref,
                    gpost_ref, ucum_ref, o_ref, k_s, vt_s, qbd_s.at[t], acc_s.at[t], uprev_s,
                    y_s.at[t], gate_s.at[t])


def _suffix_sum_matrix_t(k):
    later = np.arange(k)[None, :] > np.arange(k)[:, None]
    return jnp.asarray(np.concatenate([later, later], axis=1).astype(np.float32), dtype=BF16)


@jax.jit
def kernel(x, c, w_ada, b_ada, g_pre, w_in, w_pool, pool_scale, w_out, g_post):
    b, s, d = x.shape
    tq = TOKEN_TILE
    nsub = tq // ATT_TILE
    n_key_tiles = s // ATT_TILE
    step_rows = tq * TILES_PER_STEP
    assert s % step_rows == 0 and tq % ATT_TILE == 0 and tq >= MAX_WINDOW
    assert w_in.shape == (d, 2 * POOL_WIDTH + 4 * SB_WIDTH) and w_out.shape == (d, d)

    mod = pl.pallas_call(
        _ada_kernel,
        grid=(3 * d // ADA_COLS,),
        in_specs=[pl.BlockSpec((b, d), lambda n: (0, 0)),
                  pl.BlockSpec((d, ADA_COLS), lambda n: (0, n)),
                  pl.BlockSpec((1, ADA_COLS), lambda n: (0, n))],
        out_specs=pl.BlockSpec((b, ADA_COLS), lambda n: (0, n)),
        out_shape=jax.ShapeDtypeStruct((b, 3 * d), F32),
        name="adaln",
    )(c, w_ada, b_ada.reshape(1, 3 * d))
    mod = mod.reshape(b, 3, d)

    const = lambda *shape: pl.BlockSpec(shape, lambda bi, ti: (0,) * len(shape))
    return pl.pallas_call(
        _layer_kernel,
        grid=(b, s // step_rows),
        in_specs=[pl.BlockSpec((1, step_rows, d), lambda bi, ti: (bi, ti, 0)),
                  pl.BlockSpec((1, 3, d), lambda bi, ti: (bi, 0, 0)),
                  const(1, d),
                  const(d, w_in.shape[1]),
                  const(len(POOL_WINDOWS), POOL_GROUP, POOL_GROUP),
                  const(1, POOL_WIDTH),
                  const(d, d),
                  const(1, d),
                  const(ATT_TILE, 2 * ATT_TILE)],
        out_specs=pl.BlockSpec((1, step_rows, d), lambda bi, ti: (bi, ti, 0)),
        out_shape=jax.ShapeDtypeStruct((b, s, d), F32),
        scratch_shapes=[
            pltpu.VMEM((n_key_tiles + 1, ATT_TILE, SB_WIDTH), BF16),
            pltpu.VMEM((n_key_tiles + 1, SB_WIDTH, ATT_TILE), BF16),
            pltpu.VMEM((TILES_PER_STEP, N_PAIRS, nsub, PAIR_WIDTH, 2 * ATT_TILE), BF16),
            pltpu.VMEM((TILES_PER_STEP, N_PAIRS, nsub, PAIR_WIDTH, ATT_TILE), F32),
            pltpu.VMEM((MAX_WINDOW, POOL_WIDTH), F32),
            pltpu.VMEM((TILES_PER_STEP, tq, d), F32),
            pltpu.VMEM((TILES_PER_STEP, 2, tq, POOL_WIDTH), F32)],
        compiler_params=pltpu.CompilerParams(
            dimension_semantics=("arbitrary", "arbitrary"),
            vmem_limit_bytes=VMEM_LIMIT_BYTES),
        name="layer",
    )(x, mod, g_pre.reshape(1, d), w_in.astype(BF16), w_pool.astype(BF16),
      pool_scale.reshape(1, POOL_WIDTH), w_out.astype(BF16), g_post.reshape(1, d),
      _suffix_sum_matrix_t(ATT_TILE))
```

```python
import jax
import jax.numpy as jnp
import numpy as np
from jax import lax
from jax.experimental import pallas as pl
from jax.experimental.pallas import tpu as pltpu

F32 = jnp.float32
BF16 = jnp.bfloat16

EPS = 1e-6
POOL_WINDOWS = (2, 4, 8, 16)
POOL_GROUP = 128
HEAD_DIM = 64
N_HEADS = 8
N_PAIRS = N_HEADS // 2
PAIR_WIDTH = 2 * HEAD_DIM
SB_WIDTH = N_HEADS * HEAD_DIM
POOL_WIDTH = len(POOL_WINDOWS) * POOL_GROUP
MAX_WINDOW = max(POOL_WINDOWS)

TOKEN_TILE = 256
TILES_PER_STEP = 4
ATT_TILE = 128
MXU_COLS = 256
ADA_COLS = 512
VMEM_LIMIT_BYTES = 56 * 1024 * 1024

SKIP_BELOW = -89.0
MASKED_SCORE = -1e30


def _sigmoid(x):
    return 1.0 / (1.0 + jnp.exp(-x))


def _ada_kernel(c_ref, w_ref, b_ref, o_ref):
    c = c_ref[...]
    o_ref[...] = jnp.dot(c * _sigmoid(c), w_ref[...], preferred_element_type=F32,
                         precision=lax.Precision.HIGHEST) + b_ref[...]


def _log_terms(z):
    neg_part = jnp.minimum(z, 0.0)
    neg_pos_part = neg_part - z
    log1p_term = jnp.log(1.0 + jnp.exp(neg_part + neg_pos_part))
    return neg_part - log1p_term, neg_pos_part - log1p_term


def _hi_lo(x):
    hi = x.astype(BF16)
    lo = (x - hi.astype(F32)).astype(BF16)
    return jnp.concatenate([hi, lo], axis=0)


def _token_tile(i, rows, x_ref, mod_ref, gpre_ref, win_ref, wpool_ref, pscale_ref, wout_ref,
                gpost_ref, ucum_ref, o_ref, k_s, vt_s, qbd_s, acc_s, uprev_s, y_s, gate_s):
    tq = rows.stop - rows.start
    nsub = tq // ATT_TILE

    x = x_ref[0, rows, :]
    mod = mod_ref[0]
    shift, scale, gate = mod[0:1], mod[1:2], mod[2:3]
    hn = x * lax.rsqrt(jnp.mean(x * x, axis=-1, keepdims=True) + EPS) * gpre_ref[...]
    hb = (hn * (1.0 + scale) + shift).astype(BF16)

    def proj(first_col, width):
        return jnp.dot(hb, win_ref[:, first_col:first_col + width], preferred_element_type=F32)


    qt = (proj(2 * 512, 512) * (1.0 / np.sqrt(HEAD_DIM))).T.astype(BF16)
    k = proj(3 * 512, 512).astype(BF16)
    first_head = lax.broadcasted_iota(jnp.int32, (PAIR_WIDTH, ATT_TILE), 0) < HEAD_DIM
    zero = jnp.zeros((PAIR_WIDTH, ATT_TILE), BF16)
    for s in range(nsub):
        cols = slice(s * ATT_TILE, (s + 1) * ATT_TILE)
        k_s[i * nsub + s + 1] = k[cols, :]
        for p in range(N_PAIRS):
            qp = qt[p * PAIR_WIDTH:(p + 1) * PAIR_WIDTH, cols]
            qbd_s[p, s] = jnp.concatenate(
                [jnp.where(first_head, qp, zero), jnp.where(first_head, zero, qp)], axis=1)

    def values_chunk(half):
        def emit():
            rows = slice(half * MXU_COLS, (half + 1) * MXU_COLS)
            vt = proj(4 * 512 + half * MXU_COLS, MXU_COLS).T.astype(BF16)
            for s in range(nsub):
                vt_s[i * nsub + s + 1, rows, :] = vt[:, s * ATT_TILE:(s + 1) * ATT_TILE]
        return emit

    pooled_out = [None] * len(POOL_WINDOWS)
    u_half = [None] * (POOL_WIDTH // MXU_COLS)

    def pool_in_chunk(half):
        def emit():
            u_half[half] = proj(half * MXU_COLS, MXU_COLS)
        return emit

    def pool_chunk(half):
        def emit():
            cols = slice(half * MXU_COLS, (half + 1) * MXU_COLS)
            u = u_half[half]
            uext = jnp.concatenate([uprev_s[:, cols], u], axis=0)
            uprev_s[:, cols] = u[tq - MAX_WINDOW:, :]
            pos = i * tq + lax.broadcasted_iota(jnp.int32, (tq, 1), 0)
            for gg in range(MXU_COLS // POOL_GROUP):
                g = half * (MXU_COLS // POOL_GROUP) + gg
                w = POOL_WINDOWS[g]
                ug = uext[:, gg * POOL_GROUP:(gg + 1) * POOL_GROUP]
                acc, width = ug, 1
                while width < w:
                    acc = acc + pltpu.roll(acc, width, axis=0)
                    width *= 2
                count = jnp.minimum(pos + 1, w).astype(F32)
                pooled = acc[MAX_WINDOW:, :] / count - ug[MAX_WINDOW:, :]
                pooled_out[g] = jnp.dot(pooled.astype(BF16), wpool_ref[g],
                                        preferred_element_type=F32)
        return emit

    def gate_chunk(slab, half):
        def emit():
            cols = slice(half * MXU_COLS, (half + 1) * MXU_COLS)
            g = proj(slab * 512 + half * MXU_COLS, MXU_COLS)
            gate_s[0 if slab == 1 else 1, :, cols] = g * _sigmoid(g)
        return emit

    gated = []

    def pool_out_chunk(j):
        def emit():
            if not gated:
                y_pool = jnp.concatenate(pooled_out, axis=-1) * pscale_ref[...]
                gated.append((y_pool * gate_s[0]).astype(BF16))
            cols = slice(j * MXU_COLS, (j + 1) * MXU_COLS)
            y_s[:, cols] = jnp.dot(gated[0], wout_ref[:POOL_WIDTH, cols],
                                   preferred_element_type=F32)
        return emit

    ucum_t = ucum_ref[...]
    key_idx = lax.broadcasted_iota(jnp.int32, (ATT_TILE, 2 * ATT_TILE), 0)
    query_idx = lax.broadcasted_iota(jnp.int32, (ATT_TILE, 2 * ATT_TILE), 1) % ATT_TILE
    causal = key_idx < query_idx
    last_row = lax.broadcasted_iota(jnp.int32, (8, 2 * ATT_TILE), 0) == 7
    chains = [(s, p) for s in range(nsub) for p in range(N_PAIRS)]
    lanes = [slice(p * PAIR_WIDTH, (p + 1) * PAIR_WIDTH) for p in range(N_PAIRS)]

    def attention_step(n, carries, mask, after_sums=(), after_values=()):
        slots = [jnp.maximum(i * nsub + s - n + 1, 0) for s in range(nsub)]
        z = [jnp.dot(k_s[slots[s], :, lanes[p]], qbd_s[p, s], preferred_element_type=F32)
             for s, p in chains]
        logs, tail = [], []
        for c in range(len(chains)):
            log_beta, log_1mb = _log_terms(z[c] if mask is None else
                                           jnp.where(mask, z[c], MASKED_SCORE))
            if carries is None:
                sums = jnp.dot(ucum_t, _hi_lo(log_1mb), preferred_element_type=F32)
            else:
                on_last = jnp.where(last_row, carries[c], 0.0)
                with_carry = jnp.concatenate([log_1mb[:-8], log_1mb[-8:] + on_last], axis=0)
                sums = jnp.dot(ucum_t, _hi_lo(with_carry), preferred_element_type=F32)
                sums = jnp.concatenate([sums[:-8], sums[-8:] + on_last], axis=0)
            logs.append((log_beta, log_1mb))
            tail.append(sums)
            if c < len(after_sums):
                after_sums[c]()
        new_carries = []
        for c, (s, p) in enumerate(chains):
            log_beta, log_1mb = logs[c]
            a = jnp.exp(log_beta + tail[c])
            o = jnp.dot(vt_s[slots[s], lanes[p], :], a.astype(BF16),
                        preferred_element_type=F32)
            o = jnp.concatenate([o[:HEAD_DIM, :ATT_TILE], o[HEAD_DIM:, ATT_TILE:]], axis=0)
            if mask is not None:
                acc_s[p, s] = o
            else:
                acc_s[p, s] += o
            new_carries.append(tail[c][0:1, :] + log_1mb[0:1, :])
            if c < len(after_values):
                after_values[c]()
        return tuple(new_carries)

    def needed(n, carries):
        go = None
        for s in range(nsub):
            live = carries[s * N_PAIRS]
            for p in range(1, N_PAIRS):
                live = jnp.maximum(live, carries[s * N_PAIRS + p])
            go_s = jnp.logical_and(i * nsub + s - n >= 0, jnp.max(live) >= SKIP_BELOW)
            go = go_s if go is None else jnp.logical_or(go, go_s)
        return go

    skip = lambda: None
    carries = attention_step(
        0, None, causal,
        after_sums=[values_chunk(0), skip, values_chunk(1), skip,
                    pool_in_chunk(0), skip, pool_in_chunk(1), skip],
        after_values=[skip, gate_chunk(1, 0), skip, gate_chunk(1, 1)])
    carries = attention_step(
        1, carries, None,
        after_sums=[gate_chunk(5, 0), skip, gate_chunk(5, 1), skip,
                    pool_chunk(0), skip, pool_chunk(1), skip],
        after_values=[pool_out_chunk(j // 2) if j % 2 == 0 else skip
                      for j in range(2 * wout_ref.shape[1] // MXU_COLS)])

    def loop_body(state):
        n, _, carries = state
        carries = attention_step(n, carries, None)
        return n + 1, needed(n + 1, carries), carries

    lax.while_loop(lambda state: state[1], loop_body, (jnp.int32(2), needed(2, carries), carries))

    att_t = jnp.concatenate(
        [jnp.concatenate([acc_s[p, s] for s in range(nsub)], axis=1)
         for p in range(N_PAIRS)], axis=0)
    y_sb = att_t.T * gate_s[1]

    y = y_s[...] + jnp.dot(y_sb.astype(BF16), wout_ref[POOL_WIDTH:, :],
                           preferred_element_type=F32)
    yn = y * lax.rsqrt(jnp.mean(y * y, axis=-1, keepdims=True) + EPS) * gpost_ref[...]
    o_ref[0, rows, :] = x + gate * yn


def _layer_kernel(x_ref, mod_ref, gpre_ref, win_ref, wpool_ref, pscale_ref, wout_ref,
                  gpost_ref, ucum_ref, o_ref, k_s, vt_s, qbd_s, acc_s, uprev_s, y_s, gate_s):
    step = pl.program_id(1)

    @pl.when(step == 0)
    def _():
        uprev_s[...] = jnp.zeros_like(uprev_s)
        k_s[0] = jnp.zeros_like(k_s[0])
        vt_s[0] = jnp.zeros_like(vt_s[0])

    for t in range(TILES_PER_STEP):
        _token_tile(step * TILES_PER_STEP + t, slice(t * TOKEN_TILE, (t + 1) * TOKEN_TILE),
                    x_ref, mod_ref, gpre_ref, win_ref, wpool_ref, pscale_ref, wout_ref,
                    gpost_ref, ucum_ref, o_ref, k_s, vt_s, qbd_s.at[t], acc_s.at[t], uprev_s,
                    y_s.at[t], gate_s.at[t])


def _suffix_sum_matrix_t(k):
    later = np.arange(k)[None, :] > np.arange(k)[:, None]
    return jnp.asarray(np.concatenate([later, later], axis=1).astype(np.float32), dtype=BF16)


@jax.jit
def kernel(x, c, w_ada, b_ada, g_pre, w_in, w_pool, pool_scale, w_out, g_post):
    b, s, d = x.shape
    tq = TOKEN_TILE
    nsub = tq // ATT_TILE
    n_key_tiles = s // ATT_TILE
    step_rows = tq * TILES_PER_STEP
    assert s % step_rows == 0 and tq % ATT_TILE == 0 and tq >= MAX_WINDOW
    assert w_in.shape == (d, 2 * POOL_WIDTH + 4 * SB_WIDTH) and w_out.shape == (d, d)

    mod = pl.pallas_call(
        _ada_kernel,
        grid=(3 * d // ADA_COLS,),
        in_specs=[pl.BlockSpec((b, d), lambda n: (0, 0)),
                  pl.BlockSpec((d, ADA_COLS), lambda n: (0, n)),
                  pl.BlockSpec((1, ADA_COLS), lambda n: (0, n))],
        out_specs=pl.BlockSpec((b, ADA_COLS), lambda n: (0, n)),
        out_shape=jax.ShapeDtypeStruct((b, 3 * d), F32),
        name="adaln",
    )(c, w_ada, b_ada.reshape(1, 3 * d))
    mod = mod.reshape(b, 3, d)

    const = lambda *shape: pl.BlockSpec(shape, lambda bi, ti: (0,) * len(shape))
    return pl.pallas_call(
        _layer_kernel,
        grid=(b, s // step_rows),
        in_specs=[pl.BlockSpec((1, step_rows, d), lambda bi, ti: (bi, ti, 0)),
                  pl.BlockSpec((1, 3, d), lambda bi, ti: (bi, 0, 0)),
                  const(1, d),
                  const(d, w_in.shape[1]),
                  const(len(POOL_WINDOWS), POOL_GROUP, POOL_GROUP),
                  const(1, POOL_WIDTH),
                  const(d, d),
                  const(1, d),
                  const(ATT_TILE, 2 * ATT_TILE)],
        out_specs=pl.BlockSpec((1, step_rows, d), lambda bi, ti: (bi, ti, 0)),
        out_shape=jax.ShapeDtypeStruct((b, s, d), F32),
        scratch_shapes=[
            pltpu.VMEM((n_key_tiles + 1, ATT_TILE, SB_WIDTH), BF16),
            pltpu.VMEM((n_key_tiles + 1, SB_WIDTH, ATT_TILE), BF16),
            pltpu.VMEM((TILES_PER_STEP, N_PAIRS, nsub, PAIR_WIDTH, 2 * ATT_TILE), BF16),
            pltpu.VMEM((TILES_PER_STEP, N_PAIRS, nsub, PAIR_WIDTH, ATT_TILE), F32),
            pltpu.VMEM((MAX_WINDOW, POOL_WIDTH), F32),
            pltpu.VMEM((TILES_PER_STEP, tq, d), F32),
            pltpu.VMEM((TILES_PER_STEP, 2, tq, POOL_WIDTH), F32)],
        compiler_params=pltpu.CompilerParams(
            dimension_semantics=("arbitrary", "arbitrary"),
            vmem_limit_bytes=VMEM_LIMIT_BYTES),
        name="layer",
    )(x, mod, g_pre.reshape(1, d), w_in.astype(BF16), w_pool.astype(BF16),
      pool_scale.reshape(1, POOL_WIDTH), w_out.astype(BF16), g_post.reshape(1, d),
      _suffix_sum_matrix_t(ATT_TILE))
```

```python
import jax
import jax.numpy as jnp
import numpy as np
from jax import lax
from jax.experimental import pallas as pl
from jax.experimental.pallas import tpu as pltpu

F32 = jnp.float32
BF16 = jnp.bfloat16

EPS = 1e-6
POOL_WINDOWS = (2, 4, 8, 16)
POOL_GROUP = 128
HEAD_DIM = 64
N_HEADS = 8
N_PAIRS = N_HEADS // 2
PAIR_WIDTH = 2 * HEAD_DIM
SB_WIDTH = N_HEADS * HEAD_DIM
POOL_WIDTH = len(POOL_WINDOWS) * POOL_GROUP
MAX_WINDOW = max(POOL_WINDOWS)

TOKEN_TILE = 256
TILES_PER_STEP = 4
ATT_TILE = 128
EXTRA_KEYS = 32
MXU_COLS = 256
ADA_COLS = 512
VMEM_LIMIT_BYTES = 56 * 1024 * 1024

SKIP_BELOW = -89.0
MASKED_SCORE = -1e30


def _sigmoid(x):
    return 1.0 / (1.0 + jnp.exp(-x))


def _ada_kernel(c_ref, w_ref, b_ref, o_ref):
    c = c_ref[...]
    o_ref[...] = jnp.dot(c * _sigmoid(c), w_ref[...], preferred_element_type=F32,
                         precision=lax.Precision.HIGHEST) + b_ref[...]


def _log_terms(z):
    neg_part = jnp.minimum(z, 0.0)
    neg_pos_part = neg_part - z
    log1p_term = jnp.log(1.0 + jnp.exp(neg_part + neg_pos_part))
    return neg_part - log1p_term, neg_pos_part - log1p_term


def _hi_lo(x):
    hi = x.astype(BF16)
    lo = (x - hi.astype(F32)).astype(BF16)
    return jnp.concatenate([hi, lo], axis=0)


def _token_tile(i, rows, x_ref, mod_ref, gpre_ref, win_ref, wpool_ref, pscale_ref, wout_ref,
                gpost_ref, ucum_ref, o_ref, k_s, vt_s, qbd_s, acc_s, uprev_s, y_s, gate_s):
    tq = rows.stop - rows.start
    nsub = tq // ATT_TILE

    x = x_ref[0, rows, :]
    mod = mod_ref[0]
    shift, scale, gate = mod[0:1], mod[1:2], mod[2:3]
    hn = x * lax.rsqrt(jnp.mean(x * x, axis=-1, keepdims=True) + EPS) * gpre_ref[...]
    hb = (hn * (1.0 + scale) + shift).astype(BF16)

    def proj(first_col, width):
        return jnp.dot(hb, win_ref[:, first_col:first_col + width], preferred_element_type=F32)


    qt = (proj(2 * 512, 512) * (1.0 / np.sqrt(HEAD_DIM))).T.astype(BF16)
    k = proj(3 * 512, 512).astype(BF16)
    first_head = lax.broadcasted_iota(jnp.int32, (PAIR_WIDTH, ATT_TILE), 0) < HEAD_DIM
    zero = jnp.zeros((PAIR_WIDTH, ATT_TILE), BF16)
    for s in range(nsub):
        cols = slice(s * ATT_TILE, (s + 1) * ATT_TILE)
        k_s[i * nsub + s + 1] = k[cols, :]
        for p in range(N_PAIRS):
            qp = qt[p * PAIR_WIDTH:(p + 1) * PAIR_WIDTH, cols]
            qbd_s[p, s] = jnp.concatenate(
                [jnp.where(first_head, qp, zero), jnp.where(first_head, zero, qp)], axis=1)

    def values_chunk(half):
        def emit():
            rows = slice(half * MXU_COLS, (half + 1) * MXU_COLS)
            vt = proj(4 * 512 + half * MXU_COLS, MXU_COLS).T.astype(BF16)
            for s in range(nsub):
                vt_s[i * nsub + s + 1, rows, :] = vt[:, s * ATT_TILE:(s + 1) * ATT_TILE]
        return emit

    pooled_out = [None] * len(POOL_WINDOWS)
    u_half = [None] * (POOL_WIDTH // MXU_COLS)

    def pool_in_chunk(half):
        def emit():
            u_half[half] = proj(half * MXU_COLS, MXU_COLS)
        return emit

    def pool_chunk(half):
        def emit():
            cols = slice(half * MXU_COLS, (half + 1) * MXU_COLS)
            u = u_half[half]
            uext = jnp.concatenate([uprev_s[:, cols], u], axis=0)
            uprev_s[:, cols] = u[tq - MAX_WINDOW:, :]
            pos = i * tq + lax.broadcasted_iota(jnp.int32, (tq, 1), 0)
            for gg in range(MXU_COLS // POOL_GROUP):
                g = half * (MXU_COLS // POOL_GROUP) + gg
                w = POOL_WINDOWS[g]
                ug = uext[:, gg * POOL_GROUP:(gg + 1) * POOL_GROUP]
                acc, width = ug, 1
                while width < w:
                    acc = acc + pltpu.roll(acc, width, axis=0)
                    width *= 2
                count = jnp.minimum(pos + 1, w).astype(F32)
                pooled = acc[MAX_WINDOW:, :] / count - ug[MAX_WINDOW:, :]
                pooled_out[g] = jnp.dot(pooled.astype(BF16), wpool_ref[g],
                                        preferred_element_type=F32)
        return emit

    def gate_chunk(slab, half):
        def emit():
            cols = slice(half * MXU_COLS, (half + 1) * MXU_COLS)
            g = proj(slab * 512 + half * MXU_COLS, MXU_COLS)
            gate_s[0 if slab == 1 else 1, :, cols] = g * _sigmoid(g)
        return emit

    gated = []

    def pool_out_chunk(j):
        def emit():
            if not gated:
                y_pool = jnp.concatenate(pooled_out, axis=-1) * pscale_ref[...]
                gated.append((y_pool * gate_s[0]).astype(BF16))
            cols = slice(j * MXU_COLS, (j + 1) * MXU_COLS)
            y_s[:, cols] = jnp.dot(gated[0], wout_ref[:POOL_WIDTH, cols],
                                   preferred_element_type=F32)
        return emit

    ucum_t = ucum_ref[...]
    key_idx = lax.broadcasted_iota(jnp.int32, (ATT_TILE, 2 * ATT_TILE), 0)
    query_idx = lax.broadcasted_iota(jnp.int32, (ATT_TILE, 2 * ATT_TILE), 1) % ATT_TILE
    causal = key_idx < query_idx
    last_row = lax.broadcasted_iota(jnp.int32, (8, 2 * ATT_TILE), 0) == 7
    extra_sums_t = jnp.where(
        lax.broadcasted_iota(jnp.int32, (EXTRA_KEYS, 2 * EXTRA_KEYS), 1) % EXTRA_KEYS
        > lax.broadcasted_iota(jnp.int32, (EXTRA_KEYS, 2 * EXTRA_KEYS), 0), 1.0, 0.0).astype(BF16)
    chains = [(s, p) for s in range(nsub) for p in range(N_PAIRS)]
    lanes = [slice(p * PAIR_WIDTH, (p + 1) * PAIR_WIDTH) for p in range(N_PAIRS)]

    def attention_step(n, carries, mask, after_sums=(), after_values=(),
                       only_newest=False, without_newest=None):
        keys = slice(ATT_TILE - EXTRA_KEYS, ATT_TILE) if only_newest else slice(0, ATT_TILE)
        sums_t = extra_sums_t if only_newest else ucum_t
        slots = [jnp.maximum(i * nsub + s - n + 1, 0) for s in range(nsub)]
        z = [jnp.dot(k_s[slots[s], keys, lanes[p]], qbd_s[p, s], preferred_element_type=F32)
             for s, p in chains]
        logs, tail = [], []
        for c in range(len(chains)):
            zc = z[c] if mask is None else jnp.where(mask, z[c], MASKED_SCORE)
            if without_newest is not None:
                zc = jnp.concatenate(
                    [zc[:-EXTRA_KEYS],
                     jnp.where(without_newest, MASKED_SCORE, zc[-EXTRA_KEYS:])], axis=0)
            log_beta, log_1mb = _log_terms(zc)
            if carries is None:
                sums = jnp.dot(sums_t, _hi_lo(log_1mb), preferred_element_type=F32)
            else:
                on_last = jnp.where(last_row, carries[c], 0.0)
                with_carry = jnp.concatenate([log_1mb[:-8], log_1mb[-8:] + on_last], axis=0)
                sums = jnp.dot(sums_t, _hi_lo(with_carry), preferred_element_type=F32)
                sums = jnp.concatenate([sums[:-8], sums[-8:] + on_last], axis=0)
            logs.append((log_beta, log_1mb))
            tail.append(sums)
            if c < len(after_sums):
                after_sums[c]()
        new_carries = []
        for c, (s, p) in enumerate(chains):
            log_beta, log_1mb = logs[c]
            a = jnp.exp(log_beta + tail[c]).astype(BF16)
            if only_newest:
                a = jnp.concatenate(
                    [jnp.zeros((ATT_TILE - EXTRA_KEYS, 2 * ATT_TILE), BF16), a], axis=0)
            o = jnp.dot(vt_s[slots[s], lanes[p], :], a, preferred_element_type=F32)
            o = jnp.concatenate([o[:HEAD_DIM, :ATT_TILE], o[HEAD_DIM:, ATT_TILE:]], axis=0)
            if mask is not None:
                acc_s[p, s] = o
            else:
                acc_s[p, s] += o
            new_carries.append(tail[c][0:1, :] + log_1mb[0:1, :])
            if c < len(after_values):
                after_values[c]()
        return tuple(new_carries)

    def needed(n, carries):
        go = None
        for s in range(nsub):
            live = carries[s * N_PAIRS]
            for p in range(1, N_PAIRS):
                live = jnp.maximum(live, carries[s * N_PAIRS + p])
            go_s = jnp.logical_and(i * nsub + s - n >= 0, jnp.max(live) >= SKIP_BELOW)
            go = go_s if go is None else jnp.logical_or(go, go_s)
        return go

    skip = lambda: None
    carries = attention_step(
        0, None, causal,
        after_sums=[values_chunk(0), skip, values_chunk(1), skip,
                    pool_in_chunk(0), skip, pool_in_chunk(1), skip],
        after_values=[skip, gate_chunk(1, 0), skip, gate_chunk(1, 1)])
    carries = attention_step(
        1, carries, None,
        after_sums=[gate_chunk(5, 0), skip, gate_chunk(5, 1), skip,
                    pool_chunk(0), skip, pool_chunk(1), skip])
    carries = attention_step(
        2, carries, None, only_newest=True,
        after_sums=[pool_out_chunk(j // 2) if j % 2 == 0 else skip
                    for j in range(2 * wout_ref.shape[1] // MXU_COLS)])

    def loop_body(state):
        n, _, carries = state
        carries = attention_step(n, carries, None, without_newest=(n == 2))
        return n + 1, needed(n + 1, carries), carries

    lax.while_loop(lambda state: state[1], loop_body, (jnp.int32(2), needed(2, carries), carries))

    att_t = jnp.concatenate(
        [jnp.concatenate([acc_s[p, s] for s in range(nsub)], axis=1)
         for p in range(N_PAIRS)], axis=0)
    y_sb = att_t.T * gate_s[1]

    y = y_s[...] + jnp.dot(y_sb.astype(BF16), wout_ref[POOL_WIDTH:, :],
                           preferred_element_type=F32)
    yn = y * lax.rsqrt(jnp.mean(y * y, axis=-1, keepdims=True) + EPS) * gpost_ref[...]
    o_ref[0, rows, :] = x + gate * yn


def _layer_kernel(x_ref, mod_ref, gpre_ref, win_ref, wpool_ref, pscale_ref, wout_ref,
                  gpost_ref, ucum_ref, o_ref, k_s, vt_s, qbd_s, acc_s, uprev_s, y_s, gate_s):
    step = pl.program_id(1)

    @pl.when(step == 0)
    def _():
        uprev_s[...] = jnp.zeros_like(uprev_s)
        k_s[0] = jnp.zeros_like(k_s[0])
        vt_s[0] = jnp.zeros_like(vt_s[0])

    for t in range(TILES_PER_STEP):
        _token_tile(step * TILES_PER_STEP + t, slice(t * TOKEN_TILE, (t + 1) * TOKEN_TILE),
                    x_ref, mod_ref, gpre_ref, win_ref, wpool_ref, pscale_ref, wout_ref,
                    gpost_ref, ucum_ref, o_ref, k_s, vt_s, qbd_s.at[t], acc_s.at[t], uprev_s,
                    y_s.at[t], gate_s.at[t])


def _suffix_sum_matrix_t(k):
    later = np.arange(k)[None, :] > np.arange(k)[:, None]
    return jnp.asarray(np.concatenate([later, later], axis=1).astype(np.float32), dtype=BF16)


@jax.jit
def kernel(x, c, w_ada, b_ada, g_pre, w_in, w_pool, pool_scale, w_out, g_post):
    b, s, d = x.shape
    tq = TOKEN_TILE
    nsub = tq // ATT_TILE
    n_key_tiles = s // ATT_TILE
    step_rows = tq * TILES_PER_STEP
    assert s % step_rows == 0 and tq % ATT_TILE == 0 and tq >= MAX_WINDOW
    assert w_in.shape == (d, 2 * POOL_WIDTH + 4 * SB_WIDTH) and w_out.shape == (d, d)

    mod = pl.pallas_call(
        _ada_kernel,
        grid=(3 * d // ADA_COLS,),
        in_specs=[pl.BlockSpec((b, d), lambda n: (0, 0)),
                  pl.BlockSpec((d, ADA_COLS), lambda n: (0, n)),
                  pl.BlockSpec((1, ADA_COLS), lambda n: (0, n))],
        out_specs=pl.BlockSpec((b, ADA_COLS), lambda n: (0, n)),
        out_shape=jax.ShapeDtypeStruct((b, 3 * d), F32),
        name="adaln",
    )(c, w_ada, b_ada.reshape(1, 3 * d))
    mod = mod.reshape(b, 3, d)

    const = lambda *shape: pl.BlockSpec(shape, lambda bi, ti: (0,) * len(shape))
    return pl.pallas_call(
        _layer_kernel,
        grid=(b, s // step_rows),
        in_specs=[pl.BlockSpec((1, step_rows, d), lambda bi, ti: (bi, ti, 0)),
                  pl.BlockSpec((1, 3, d), lambda bi, ti: (bi, 0, 0)),
                  const(1, d),
                  const(d, w_in.shape[1]),
                  const(len(POOL_WINDOWS), POOL_GROUP, POOL_GROUP),
                  const(1, POOL_WIDTH),
                  const(d, d),
                  const(1, d),
                  const(ATT_TILE, 2 * ATT_TILE)],
        out_specs=pl.BlockSpec((1, step_rows, d), lambda bi, ti: (bi, ti, 0)),
        out_shape=jax.ShapeDtypeStruct((b, s, d), F32),
        scratch_shapes=[
            pltpu.VMEM((n_key_tiles + 1, ATT_TILE, SB_WIDTH), BF16),
            pltpu.VMEM((n_key_tiles + 1, SB_WIDTH, ATT_TILE), BF16),
            pltpu.VMEM((TILES_PER_STEP, N_PAIRS, nsub, PAIR_WIDTH, 2 * ATT_TILE), BF16),
            pltpu.VMEM((TILES_PER_STEP, N_PAIRS, nsub, PAIR_WIDTH, ATT_TILE), F32),
            pltpu.VMEM((MAX_WINDOW, POOL_WIDTH), F32),
            pltpu.VMEM((TILES_PER_STEP, tq, d), F32),
            pltpu.VMEM((TILES_PER_STEP, 2, tq, POOL_WIDTH), F32)],
        compiler_params=pltpu.CompilerParams(
            dimension_semantics=("arbitrary", "arbitrary"),
            vmem_limit_bytes=VMEM_LIMIT_BYTES),
        name="layer",
    )(x, mod, g_pre.reshape(1, d), w_in.astype(BF16), w_pool.astype(BF16),
      pool_scale.reshape(1, POOL_WIDTH), w_out.astype(BF16), g_post.reshape(1, d),
      _suffix_sum_matrix_t(ATT_TILE))
```

```python
import jax
import jax.numpy as jnp
import numpy as np
from jax import lax
from jax.experimental import pallas as pl
from jax.experimental.pallas import tpu as pltpu

F32 = jnp.float32
BF16 = jnp.bfloat16

EPS = 1e-6
POOL_WINDOWS = (2, 4, 8, 16)
POOL_GROUP = 128
HEAD_DIM = 64
N_HEADS = 8
N_PAIRS = N_HEADS // 2
PAIR_WIDTH = 2 * HEAD_DIM
SB_WIDTH = N_HEADS * HEAD_DIM
POOL_WIDTH = len(POOL_WINDOWS) * POOL_GROUP
MAX_WINDOW = max(POOL_WINDOWS)

TOKEN_TILE = 256
TILES_PER_STEP = 4
ATT_TILE = 128
EXTRA_KEYS = 32
MXU_COLS = 256
ADA_COLS = 512
VMEM_LIMIT_BYTES = 56 * 1024 * 1024

SKIP_BELOW = -89.0
MASKED_SCORE = -1e30


def _sigmoid(x):
    return 1.0 / (1.0 + jnp.exp(-x))


def _ada_kernel(c_ref, w_ref, b_ref, o_ref):
    c = c_ref[...]
    o_ref[...] = jnp.dot(c * _sigmoid(c), w_ref[...], preferred_element_type=F32,
                         precision=lax.Precision.HIGHEST) + b_ref[...]


def _log_terms(z):
    neg_part = jnp.minimum(z, 0.0)
    neg_pos_part = neg_part - z
    log1p_term = jnp.log(1.0 + jnp.exp(neg_part + neg_pos_part))
    return neg_part - log1p_term, neg_pos_part - log1p_term


def _hi_lo(x):
    hi = x.astype(BF16)
    lo = (x - hi.astype(F32)).astype(BF16)
    return jnp.concatenate([hi, lo], axis=0)


def _token_tile(i, rows, x_ref, mod_ref, gpre_ref, win_ref, wpool_ref, pscale_ref, wout_ref,
                gpost_ref, ucum_ref, o_ref, k_s, vt_s, qbd_s, acc_s, uprev_s, y_s, gate_s):
    tq = rows.stop - rows.start
    nsub = tq // ATT_TILE

    x = x_ref[0, rows, :]
    mod = mod_ref[0]
    shift, scale, gate = mod[0:1], mod[1:2], mod[2:3]
    hn = x * lax.rsqrt(jnp.mean(x * x, axis=-1, keepdims=True) + EPS) * gpre_ref[...]
    hb = (hn * (1.0 + scale) + shift).astype(BF16)

    def proj(first_col, width):
        return jnp.dot(hb, win_ref[:, first_col:first_col + width], preferred_element_type=F32)


    qt = (proj(2 * 512, 512) * (1.0 / np.sqrt(HEAD_DIM))).T.astype(BF16)
    k = proj(3 * 512, 512).astype(BF16)
    first_head = lax.broadcasted_iota(jnp.int32, (PAIR_WIDTH, ATT_TILE), 0) < HEAD_DIM
    zero = jnp.zeros((PAIR_WIDTH, ATT_TILE), BF16)
    for s in range(nsub):
        cols = slice(s * ATT_TILE, (s + 1) * ATT_TILE)
        k_s[i * nsub + s + 1] = k[cols, :]
        for p in range(N_PAIRS):
            qp = qt[p * PAIR_WIDTH:(p + 1) * PAIR_WIDTH, cols]
            qbd_s[p, s] = jnp.concatenate(
                [jnp.where(first_head, qp, zero), jnp.where(first_head, zero, qp)], axis=1)

    def values_chunk(half):
        def emit():
            rows = slice(half * MXU_COLS, (half + 1) * MXU_COLS)
            vt = proj(4 * 512 + half * MXU_COLS, MXU_COLS).T.astype(BF16)
            for s in range(nsub):
                vt_s[i * nsub + s + 1, rows, :] = vt[:, s * ATT_TILE:(s + 1) * ATT_TILE]
        return emit

    pooled_out = [None] * len(POOL_WINDOWS)
    u_half = [None] * (POOL_WIDTH // MXU_COLS)

    def pool_in_chunk(half):
        def emit():
            u_half[half] = proj(half * MXU_COLS, MXU_COLS)
        return emit

    def pool_chunk(half):
        def emit():
            cols = slice(half * MXU_COLS, (half + 1) * MXU_COLS)
            u = u_half[half]
            uext = jnp.concatenate([uprev_s[:, cols], u], axis=0)
            uprev_s[:, cols] = u[tq - MAX_WINDOW:, :]
            pos = i * tq + lax.broadcasted_iota(jnp.int32, (tq, 1), 0)
            for gg in range(MXU_COLS // POOL_GROUP):
                g = half * (MXU_COLS // POOL_GROUP) + gg
                w = POOL_WINDOWS[g]
                ug = uext[:, gg * POOL_GROUP:(gg + 1) * POOL_GROUP]
                acc, width = ug, 1
                while width < w:
                    acc = acc + pltpu.roll(acc, width, axis=0)
                    width *= 2
                count = jnp.minimum(pos + 1, w).astype(F32)
                pooled = acc[MAX_WINDOW:, :] / count - ug[MAX_WINDOW:, :]
                pooled_out[g] = jnp.dot(pooled.astype(BF16), wpool_ref[g],
                                        preferred_element_type=F32)
        return emit

    def gate_chunk(slab, half):
        def emit():
            cols = slice(half * MXU_COLS, (half + 1) * MXU_COLS)
            g = proj(slab * 512 + half * MXU_COLS, MXU_COLS)
            gate_s[0 if slab == 1 else 1, :, cols] = g * _sigmoid(g)
        return emit

    gated = []

    def pool_out_chunk(j):
        def emit():
            if not gated:
                y_pool = jnp.concatenate(pooled_out, axis=-1) * pscale_ref[...]
                gated.append((y_pool * gate_s[0]).astype(BF16))
            cols = slice(j * MXU_COLS, (j + 1) * MXU_COLS)
            y_s[:, cols] = jnp.dot(gated[0], wout_ref[:POOL_WIDTH, cols],
                                   preferred_element_type=F32)
        return emit

    ucum_t = ucum_ref[...]
    key_idx = lax.broadcasted_iota(jnp.int32, (ATT_TILE, 2 * ATT_TILE), 0)
    query_idx = lax.broadcasted_iota(jnp.int32, (ATT_TILE, 2 * ATT_TILE), 1) % ATT_TILE
    causal = key_idx < query_idx
    last_row = lax.broadcasted_iota(jnp.int32, (8, 2 * ATT_TILE), 0) == 7
    extra_sums_t = jnp.where(
        lax.broadcasted_iota(jnp.int32, (EXTRA_KEYS, 2 * EXTRA_KEYS), 1) % EXTRA_KEYS
        > lax.broadcasted_iota(jnp.int32, (EXTRA_KEYS, 2 * EXTRA_KEYS), 0), 1.0, 0.0).astype(BF16)
    chains = [(s, p) for s in range(nsub) for p in range(N_PAIRS)]
    lanes = [slice(p * PAIR_WIDTH, (p + 1) * PAIR_WIDTH) for p in range(N_PAIRS)]

    def slots_of(n):
        return [jnp.maximum(i * nsub + s - n + 1, 0) for s in range(nsub)]

    def run(hooks, c):
        if c < len(hooks):
            hooks[c]()


    def score_logs(n, mask=None, newest=False, without_newest=None):
        keys = slice(ATT_TILE - EXTRA_KEYS, ATT_TILE) if newest else slice(0, ATT_TILE)
        slots = slots_of(n)
        z = [jnp.dot(k_s[slots[s], keys, lanes[p]], qbd_s[p, s], preferred_element_type=F32)
             for s, p in chains]
        logs = []
        for zc in z:
            if mask is not None:
                zc = jnp.where(mask, zc, MASKED_SCORE)
            if without_newest is not None:
                zc = jnp.concatenate(
                    [zc[:-EXTRA_KEYS],
                     jnp.where(without_newest, MASKED_SCORE, zc[-EXTRA_KEYS:])], axis=0)
            logs.append(_log_terms(zc))
        return logs

    def suffix_sums(logs, carries, newest=False, after=()):
        sums_t = extra_sums_t if newest else ucum_t
        tails, new_carries = [], []
        for c, (_, log_1mb) in enumerate(logs):
            if carries is None:
                sums = jnp.dot(sums_t, _hi_lo(log_1mb), preferred_element_type=F32)
            else:
                on_last = jnp.where(last_row, carries[c], 0.0)
                with_carry = jnp.concatenate([log_1mb[:-8], log_1mb[-8:] + on_last], axis=0)
                sums = jnp.dot(sums_t, _hi_lo(with_carry), preferred_element_type=F32)
                sums = jnp.concatenate([sums[:-8], sums[-8:] + on_last], axis=0)
            tails.append(sums)
            new_carries.append(sums[0:1, :] + log_1mb[0:1, :])
            run(after, c)
        return tails, tuple(new_carries)

    def weights(logs, tails, c, newest=False):
        a = jnp.exp(logs[c][0] + tails[c]).astype(BF16)
        if newest:
            a = jnp.concatenate(
                [jnp.zeros((ATT_TILE - EXTRA_KEYS, 2 * ATT_TILE), BF16), a], axis=0)
        return a

    def weighted_values(steps, a, c, first):
        s, p = chains[c]
        v_t = jnp.concatenate([vt_s[slots_of(n)[s], lanes[p], :] for n in steps], axis=1)
        o = jnp.dot(v_t, jnp.concatenate(a, axis=0), preferred_element_type=F32)
        o = jnp.concatenate([o[:HEAD_DIM, :ATT_TILE], o[HEAD_DIM:, ATT_TILE:]], axis=0)
        if first:
            acc_s[p, s] = o
        else:
            acc_s[p, s] += o

    def needed(n, carries):
        go = None
        for s in range(nsub):
            live = carries[s * N_PAIRS]
            for p in range(1, N_PAIRS):
                live = jnp.maximum(live, carries[s * N_PAIRS + p])
            go_s = jnp.logical_and(i * nsub + s - n >= 0, jnp.max(live) >= SKIP_BELOW)
            go = go_s if go is None else jnp.logical_or(go, go_s)
        return go

    skip = lambda: None
    logs0 = score_logs(0, mask=causal)
    tails0, carries = suffix_sums(
        logs0, None, after=[values_chunk(0), skip, values_chunk(1), skip,
                            pool_in_chunk(0), skip, pool_in_chunk(1), skip])
    a0 = []
    gate_hooks = [skip, gate_chunk(1, 0), skip, gate_chunk(1, 1)]
    for c in range(len(chains)):
        a0.append(weights(logs0, tails0, c))
        run(gate_hooks, c)
    logs1 = score_logs(1)
    logs2 = score_logs(2, newest=True)
    tails1, carries = suffix_sums(
        logs1, carries, after=[gate_chunk(5, 0), skip, gate_chunk(5, 1), skip,
                               pool_chunk(0), skip, pool_chunk(1), skip])
    tails2, carries = suffix_sums(logs2, carries, newest=True)
    out_hooks = [pool_out_chunk(j // 2) if j % 2 == 0 else skip
                 for j in range(2 * wout_ref.shape[1] // MXU_COLS)]
    for c in range(len(chains)):
        weighted_values((0, 1, 2), [a0[c], weights(logs1, tails1, c),
                                    weights(logs2, tails2, c, newest=True)], c, first=True)
        run(out_hooks, c)

    def loop_body(state):
        n, _, carries = state
        logs = score_logs(n, without_newest=(n == 2))
        tails, carries = suffix_sums(logs, carries)
        for c in range(len(chains)):
            weighted_values((n,), [weights(logs, tails, c)], c, first=False)
        return n + 1, needed(n + 1, carries), carries

    lax.while_loop(lambda state: state[1], loop_body, (jnp.int32(2), needed(2, carries), carries))

    att_t = jnp.concatenate(
        [jnp.concatenate([acc_s[p, s] for s in range(nsub)], axis=1)
         for p in range(N_PAIRS)], axis=0)
    y_sb = att_t.T * gate_s[1]

    y = y_s[...] + jnp.dot(y_sb.astype(BF16), wout_ref[POOL_WIDTH:, :],
                           preferred_element_type=F32)
    yn = y * lax.rsqrt(jnp.mean(y * y, axis=-1, keepdims=True) + EPS) * gpost_ref[...]
    o_ref[0, rows, :] = x + gate * yn


def _layer_kernel(x_ref, mod_ref, gpre_ref, win_ref, wpool_ref, pscale_ref, wout_ref,
                  gpost_ref, ucum_ref, o_ref, k_s, vt_s, qbd_s, acc_s, uprev_s, y_s, gate_s):
    step = pl.program_id(1)

    @pl.when(step == 0)
    def _():
        uprev_s[...] = jnp.zeros_like(uprev_s)
        k_s[0] = jnp.zeros_like(k_s[0])
        vt_s[0] = jnp.zeros_like(vt_s[0])

    for t in range(TILES_PER_STEP):
        _token_tile(step * TILES_PER_STEP + t, slice(t * TOKEN_TILE, (t + 1) * TOKEN_TILE),
                    x_ref, mod_ref, gpre_ref, win_ref, wpool_ref, pscale_ref, wout_ref,
                    gpost_ref, ucum_ref, o_ref, k_s, vt_s, qbd_s.at[t], acc_s.at[t], uprev_s,
                    y_s.at[t], gate_s.at[t])


def _suffix_sum_matrix_t(k):
    later = np.arange(k)[None, :] > np.arange(k)[:, None]
    return jnp.asarray(np.concatenate([later, later], axis=1).astype(np.float32), dtype=BF16)


@jax.jit
def kernel(x, c, w_ada, b_ada, g_pre, w_in, w_pool, pool_scale, w_out, g_post):
    b, s, d = x.shape
    tq = TOKEN_TILE
    nsub = tq // ATT_TILE
    n_key_tiles = s // ATT_TILE
    step_rows = tq * TILES_PER_STEP
    assert s % step_rows == 0 and tq % ATT_TILE == 0 and tq >= MAX_WINDOW
    assert w_in.shape == (d, 2 * POOL_WIDTH + 4 * SB_WIDTH) and w_out.shape == (d, d)

    mod = pl.pallas_call(
        _ada_kernel,
        grid=(3 * d // ADA_COLS,),
        in_specs=[pl.BlockSpec((b, d), lambda n: (0, 0)),
                  pl.BlockSpec((d, ADA_COLS), lambda n: (0, n)),
                  pl.BlockSpec((1, ADA_COLS), lambda n: (0, n))],
        out_specs=pl.BlockSpec((b, ADA_COLS), lambda n: (0, n)),
        out_shape=jax.ShapeDtypeStruct((b, 3 * d), F32),
        name="adaln",
    )(c, w_ada, b_ada.reshape(1, 3 * d))
    mod = mod.reshape(b, 3, d)

    const = lambda *shape: pl.BlockSpec(shape, lambda bi, ti: (0,) * len(shape))
    return pl.pallas_call(
        _layer_kernel,
        grid=(b, s // step_rows),
        in_specs=[pl.BlockSpec((1, step_rows, d), lambda bi, ti: (bi, ti, 0)),
                  pl.BlockSpec((1, 3, d), lambda bi, ti: (bi, 0, 0)),
                  const(1, d),
                  const(d, w_in.shape[1]),
                  const(len(POOL_WINDOWS), POOL_GROUP, POOL_GROUP),
                  const(1, POOL_WIDTH),
                  const(d, d),
                  const(1, d),
                  const(ATT_TILE, 2 * ATT_TILE)],
        out_specs=pl.BlockSpec((1, step_rows, d), lambda bi, ti: (bi, ti, 0)),
        out_shape=jax.ShapeDtypeStruct((b, s, d), F32),
        scratch_shapes=[
            pltpu.VMEM((n_key_tiles + 1, ATT_TILE, SB_WIDTH), BF16),
            pltpu.VMEM((n_key_tiles + 1, SB_WIDTH, ATT_TILE), BF16),
            pltpu.VMEM((TILES_PER_STEP, N_PAIRS, nsub, PAIR_WIDTH, 2 * ATT_TILE), BF16),
            pltpu.VMEM((TILES_PER_STEP, N_PAIRS, nsub, PAIR_WIDTH, ATT_TILE), F32),
            pltpu.VMEM((MAX_WINDOW, POOL_WIDTH), F32),
            pltpu.VMEM((TILES_PER_STEP, tq, d), F32),
            pltpu.VMEM((TILES_PER_STEP, 2, tq, POOL_WIDTH), F32)],
        compiler_params=pltpu.CompilerParams(
            dimension_semantics=("arbitrary", "arbitrary"),
            vmem_limit_bytes=VMEM_LIMIT_BYTES),
        name="layer",
    )(x, mod, g_pre.reshape(1, d), w_in.astype(BF16), w_pool.astype(BF16),
      pool_scale.reshape(1, POOL_WIDTH), w_out.astype(BF16), g_post.reshape(1, d),
      _suffix_sum_matrix_t(ATT_TILE))
```

```python
import jax
import jax.numpy as jnp
import numpy as np
from jax import lax
from jax.experimental import pallas as pl
from jax.experimental.pallas import tpu as pltpu

F32 = jnp.float32
BF16 = jnp.bfloat16

EPS = 1e-6
POOL_WINDOWS = (2, 4, 8, 16)
POOL_GROUP = 128
HEAD_DIM = 64
N_HEADS = 8
N_PAIRS = N_HEADS // 2
PAIR_WIDTH = 2 * HEAD_DIM
SB_WIDTH = N_HEADS * HEAD_DIM
POOL_WIDTH = len(POOL_WINDOWS) * POOL_GROUP
MAX_WINDOW = max(POOL_WINDOWS)

TOKEN_TILE = 256
TILES_PER_STEP = 4
ATT_TILE = 128
EXTRA_KEYS = 16
MXU_COLS = 256
ADA_COLS = 512
VMEM_LIMIT_BYTES = 56 * 1024 * 1024

SKIP_BELOW = -89.0
MASKED_SCORE = -1e30


def _sigmoid(x):
    return 1.0 / (1.0 + jnp.exp(-x))


def _ada_kernel(c_ref, w_ref, b_ref, o_ref):
    c = c_ref[...]
    o_ref[...] = jnp.dot(c * _sigmoid(c), w_ref[...], preferred_element_type=F32,
                         precision=lax.Precision.HIGHEST) + b_ref[...]


def _log_terms(z):
    neg_part = jnp.minimum(z, 0.0)
    neg_pos_part = neg_part - z
    log1p_term = jnp.log(1.0 + jnp.exp(neg_part + neg_pos_part))
    return neg_part - log1p_term, neg_pos_part - log1p_term


def _hi_lo(x):
    hi = x.astype(BF16)
    lo = (x - hi.astype(F32)).astype(BF16)
    return jnp.concatenate([hi, lo], axis=0)


def _token_tile(i, rows, x_ref, mod_ref, gpre_ref, win_ref, wpool_ref, pscale_ref, wout_ref,
                gpost_ref, ucum_ref, o_ref, k_s, vt_s, qbd_s, acc_s, uprev_s, y_s, gate_s):
    tq = rows.stop - rows.start
    nsub = tq // ATT_TILE

    x = x_ref[0, rows, :]
    mod = mod_ref[0]
    shift, scale, gate = mod[0:1], mod[1:2], mod[2:3]
    hn = x * lax.rsqrt(jnp.mean(x * x, axis=-1, keepdims=True) + EPS)
    hb = (hn * (gpre_ref[...] * (1.0 + scale)) + shift).astype(BF16)

    def proj(first_col, width):
        return jnp.dot(hb, win_ref[:, first_col:first_col + width], preferred_element_type=F32)


    qt = (proj(2 * 512, 512) * (1.0 / np.sqrt(HEAD_DIM))).T.astype(BF16)
    k = proj(3 * 512, 512).astype(BF16)
    first_head = lax.broadcasted_iota(jnp.int32, (PAIR_WIDTH, ATT_TILE), 0) < HEAD_DIM
    zero = jnp.zeros((PAIR_WIDTH, ATT_TILE), BF16)
    for s in range(nsub):
        cols = slice(s * ATT_TILE, (s + 1) * ATT_TILE)
        k_s[i * nsub + s + 1] = k[cols, :]
        for p in range(N_PAIRS):
            qp = qt[p * PAIR_WIDTH:(p + 1) * PAIR_WIDTH, cols]
            qbd_s[p, s] = jnp.concatenate(
                [jnp.where(first_head, qp, zero), jnp.where(first_head, zero, qp)], axis=1)

    def values_chunk(half):
        def emit():
            rows = slice(half * MXU_COLS, (half + 1) * MXU_COLS)
            vt = proj(4 * 512 + half * MXU_COLS, MXU_COLS).T.astype(BF16)
            for s in range(nsub):
                vt_s[i * nsub + s + 1, rows, :] = vt[:, s * ATT_TILE:(s + 1) * ATT_TILE]
        return emit

    pooled_out = [None] * len(POOL_WINDOWS)
    u_half = [None] * (POOL_WIDTH // MXU_COLS)

    def pool_in_chunk(half):
        def emit():
            u_half[half] = proj(half * MXU_COLS, MXU_COLS)
        return emit

    def pool_chunk(half):
        def emit():
            cols = slice(half * MXU_COLS, (half + 1) * MXU_COLS)
            u = u_half[half]
            uext = jnp.concatenate([uprev_s[:, cols], u], axis=0)
            uprev_s[:, cols] = u[tq - MAX_WINDOW:, :]
            pos = i * tq + lax.broadcasted_iota(jnp.int32, (tq, 1), 0)
            for gg in range(MXU_COLS // POOL_GROUP):
                g = half * (MXU_COLS // POOL_GROUP) + gg
                w = POOL_WINDOWS[g]
                ug = uext[:, gg * POOL_GROUP:(gg + 1) * POOL_GROUP]
                acc, width = ug, 1
                while width < w:
                    acc = acc + pltpu.roll(acc, width, axis=0)
                    width *= 2
                count = jnp.minimum(pos + 1, w).astype(F32)
                pooled = acc[MAX_WINDOW:, :] / count - ug[MAX_WINDOW:, :]
                pooled_out[g] = jnp.dot(pooled.astype(BF16), wpool_ref[g],
                                        preferred_element_type=F32)
        return emit

    def gate_chunk(slab, half):
        def emit():
            cols = slice(half * MXU_COLS, (half + 1) * MXU_COLS)
            g = proj(slab * 512 + half * MXU_COLS, MXU_COLS)
            gate_s[0 if slab == 1 else 1, :, cols] = g * _sigmoid(g)
        return emit

    gated = []

    def pool_out_chunk(j):
        def emit():
            if not gated:
                y_pool = jnp.concatenate(pooled_out, axis=-1) * pscale_ref[...]
                gated.append((y_pool * gate_s[0]).astype(BF16))
            cols = slice(j * MXU_COLS, (j + 1) * MXU_COLS)
            y_s[:, cols] = jnp.dot(gated[0], wout_ref[:POOL_WIDTH, cols],
                                   preferred_element_type=F32)
        return emit

    ucum_t = ucum_ref[...]
    key_idx = lax.broadcasted_iota(jnp.int32, (ATT_TILE, 2 * ATT_TILE), 0)
    query_idx = lax.broadcasted_iota(jnp.int32, (ATT_TILE, 2 * ATT_TILE), 1) % ATT_TILE
    causal = key_idx < query_idx
    last_row = lax.broadcasted_iota(jnp.int32, (8, 2 * ATT_TILE), 0) == 7
    extra_sums_t = jnp.where(
        lax.broadcasted_iota(jnp.int32, (EXTRA_KEYS, 2 * EXTRA_KEYS), 1) % EXTRA_KEYS
        > lax.broadcasted_iota(jnp.int32, (EXTRA_KEYS, 2 * EXTRA_KEYS), 0), 1.0, 0.0).astype(BF16)
    chains = [(s, p) for s in range(nsub) for p in range(N_PAIRS)]
    lanes = [slice(p * PAIR_WIDTH, (p + 1) * PAIR_WIDTH) for p in range(N_PAIRS)]

    def slots_of(n):
        return [jnp.maximum(i * nsub + s - n + 1, 0) for s in range(nsub)]

    def run(hooks, c):
        if c < len(hooks):
            hooks[c]()


    def score_logs(n, mask=None, newest=False, without_newest=None):
        keys = slice(ATT_TILE - EXTRA_KEYS, ATT_TILE) if newest else slice(0, ATT_TILE)
        slots = slots_of(n)
        z = [jnp.dot(k_s[slots[s], keys, lanes[p]], qbd_s[p, s], preferred_element_type=F32)
             for s, p in chains]
        logs = []
        for zc in z:
            if mask is not None:
                zc = jnp.where(mask, zc, MASKED_SCORE)
            if without_newest is not None:
                zc = jnp.concatenate(
                    [zc[:-EXTRA_KEYS],
                     jnp.where(without_newest, MASKED_SCORE, zc[-EXTRA_KEYS:])], axis=0)
            logs.append(_log_terms(zc))
        return logs

    def suffix_sums(logs, carries, newest=False, after=()):
        sums_t = extra_sums_t if newest else ucum_t
        tails, new_carries = [], []
        for c, (_, log_1mb) in enumerate(logs):
            if carries is None:
                sums = jnp.dot(sums_t, _hi_lo(log_1mb), preferred_element_type=F32)
            else:
                on_last = jnp.where(last_row, carries[c], 0.0)
                with_carry = jnp.concatenate([log_1mb[:-8], log_1mb[-8:] + on_last], axis=0)
                sums = jnp.dot(sums_t, _hi_lo(with_carry), preferred_element_type=F32)
                sums = jnp.concatenate([sums[:-8], sums[-8:] + on_last], axis=0)
            tails.append(sums)
            new_carries.append(sums[0:1, :] + log_1mb[0:1, :])
            run(after, c)
        return tails, tuple(new_carries)

    def weights(logs, tails, c, newest=False):
        a = jnp.exp(logs[c][0] + tails[c]).astype(BF16)
        if newest:
            a = jnp.concatenate(
                [jnp.zeros((ATT_TILE - EXTRA_KEYS, 2 * ATT_TILE), BF16), a], axis=0)
        return a

    def weighted_values(steps, a, c, first):
        s, p = chains[c]
        v_t = jnp.concatenate([vt_s[slots_of(n)[s], lanes[p], :] for n in steps], axis=1)
        o = jnp.dot(v_t, jnp.concatenate(a, axis=0), preferred_element_type=F32)
        o = jnp.concatenate([o[:HEAD_DIM, :ATT_TILE], o[HEAD_DIM:, ATT_TILE:]], axis=0)
        if first:
            acc_s[p, s] = o
        else:
            acc_s[p, s] += o

    def needed(n, carries):
        go = None
        for s in range(nsub):
            live = carries[s * N_PAIRS]
            for p in range(1, N_PAIRS):
                live = jnp.maximum(live, carries[s * N_PAIRS + p])
            go_s = jnp.logical_and(i * nsub + s - n >= 0, jnp.max(live) >= SKIP_BELOW)
            go = go_s if go is None else jnp.logical_or(go, go_s)
        return go

    skip = lambda: None
    logs0 = score_logs(0, mask=causal)
    tails0, carries = suffix_sums(
        logs0, None, after=[values_chunk(0), skip, values_chunk(1), skip,
                            pool_in_chunk(0), skip, pool_in_chunk(1), skip])
    a0 = []
    gate_hooks = [skip, gate_chunk(1, 0), skip, gate_chunk(1, 1)]
    for c in range(len(chains)):
        a0.append(weights(logs0, tails0, c))
        run(gate_hooks, c)
    logs1 = score_logs(1)
    logs2 = score_logs(2, newest=True)
    tails1, carries = suffix_sums(
        logs1, carries, after=[gate_chunk(5, 0), skip, gate_chunk(5, 1), skip,
                               pool_chunk(0), skip, pool_chunk(1), skip])
    tails2, carries = suffix_sums(logs2, carries, newest=True)
    out_hooks = [pool_out_chunk(j // 2) if j % 2 == 0 else skip
                 for j in range(2 * wout_ref.shape[1] // MXU_COLS)]
    for c in range(len(chains)):
        weighted_values((0, 1, 2), [a0[c], weights(logs1, tails1, c),
                                    weights(logs2, tails2, c, newest=True)], c, first=True)
        run(out_hooks, c)

    def loop_body(state):
        n, _, carries = state
        logs = score_logs(n, without_newest=(n == 2))
        tails, carries = suffix_sums(logs, carries)
        for c in range(len(chains)):
            weighted_values((n,), [weights(logs, tails, c)], c, first=False)
        return n + 1, needed(n + 1, carries), carries

    def finish():
        att_t = jnp.concatenate(
            [jnp.concatenate([acc_s[p, s] for s in range(nsub)], axis=1)
             for p in range(N_PAIRS)], axis=0)
        y_sb = att_t.T * gate_s[1]
        y = y_s[...] + jnp.dot(y_sb.astype(BF16), wout_ref[POOL_WIDTH:, :],
                               preferred_element_type=F32)
        yn = y * lax.rsqrt(jnp.mean(y * y, axis=-1, keepdims=True) + EPS)
        o_ref[0, rows, :] = x_ref[0, rows, :] + yn * (gate * gpost_ref[...])

    finish()

    def further_steps():
        lax.while_loop(lambda state: state[1], loop_body, (jnp.int32(2), jnp.bool_(True), carries))
        finish()

    return needed(2, carries), further_steps


def _layer_kernel(x_ref, mod_ref, gpre_ref, win_ref, wpool_ref, pscale_ref, wout_ref,
                  gpost_ref, ucum_ref, o_ref, k_s, vt_s, qbd_s, acc_s, uprev_s, y_s, gate_s):
    step = pl.program_id(1)

    @pl.when(step == 0)
    def _():
        uprev_s[...] = jnp.zeros_like(uprev_s)
        k_s[0] = jnp.zeros_like(k_s[0])
        vt_s[0] = jnp.zeros_like(vt_s[0])

    unfinished = [
        _token_tile(step * TILES_PER_STEP + t, slice(t * TOKEN_TILE, (t + 1) * TOKEN_TILE),
                    x_ref, mod_ref, gpre_ref, win_ref, wpool_ref, pscale_ref, wout_ref,
                    gpost_ref, ucum_ref, o_ref, k_s, vt_s, qbd_s.at[t], acc_s.at[t], uprev_s,
                    y_s.at[t], gate_s.at[t])
        for t in range(TILES_PER_STEP)]
    for more_needed, further_steps in unfinished:
        pl.when(more_needed)(further_steps)


def _suffix_sum_matrix_t(k):
    later = np.arange(k)[None, :] > np.arange(k)[:, None]
    return jnp.asarray(np.concatenate([later, later], axis=1).astype(np.float32), dtype=BF16)


@jax.jit
def kernel(x, c, w_ada, b_ada, g_pre, w_in, w_pool, pool_scale, w_out, g_post):
    b, s, d = x.shape
    tq = TOKEN_TILE
    nsub = tq // ATT_TILE
    n_key_tiles = s // ATT_TILE
    step_rows = tq * TILES_PER_STEP
    assert s % step_rows == 0 and tq % ATT_TILE == 0 and tq >= MAX_WINDOW
    assert w_in.shape == (d, 2 * POOL_WIDTH + 4 * SB_WIDTH) and w_out.shape == (d, d)

    mod = pl.pallas_call(
        _ada_kernel,
        grid=(3 * d // ADA_COLS,),
        in_specs=[pl.BlockSpec((b, d), lambda n: (0, 0)),
                  pl.BlockSpec((d, ADA_COLS), lambda n: (0, n)),
                  pl.BlockSpec((1, ADA_COLS), lambda n: (0, n))],
        out_specs=pl.BlockSpec((b, ADA_COLS), lambda n: (0, n)),
        out_shape=jax.ShapeDtypeStruct((b, 3 * d), F32),
        name="adaln",
    )(c, w_ada, b_ada.reshape(1, 3 * d))
    mod = mod.reshape(b, 3, d)

    const = lambda *shape: pl.BlockSpec(shape, lambda bi, ti: (0,) * len(shape))
    return pl.pallas_call(
        _layer_kernel,
        grid=(b, s // step_rows),
        in_specs=[pl.BlockSpec((1, step_rows, d), lambda bi, ti: (bi, ti, 0)),
                  pl.BlockSpec((1, 3, d), lambda bi, ti: (bi, 0, 0)),
                  const(1, d),
                  const(d, w_in.shape[1]),
                  const(len(POOL_WINDOWS), POOL_GROUP, POOL_GROUP),
                  const(1, POOL_WIDTH),
                  const(d, d),
                  const(1, d),
                  const(ATT_TILE, 2 * ATT_TILE)],
        out_specs=pl.BlockSpec((1, step_rows, d), lambda bi, ti: (bi, ti, 0)),
        out_shape=jax.ShapeDtypeStruct((b, s, d), F32),
        scratch_shapes=[
            pltpu.VMEM((n_key_tiles + 1, ATT_TILE, SB_WIDTH), BF16),
            pltpu.VMEM((n_key_tiles + 1, SB_WIDTH, ATT_TILE), BF16),
            pltpu.VMEM((TILES_PER_STEP, N_PAIRS, nsub, PAIR_WIDTH, 2 * ATT_TILE), BF16),
            pltpu.VMEM((TILES_PER_STEP, N_PAIRS, nsub, PAIR_WIDTH, ATT_TILE), F32),
            pltpu.VMEM((MAX_WINDOW, POOL_WIDTH), F32),
            pltpu.VMEM((TILES_PER_STEP, tq, d), F32),
            pltpu.VMEM((TILES_PER_STEP, 2, tq, POOL_WIDTH), F32)],
        compiler_params=pltpu.CompilerParams(
            dimension_semantics=("arbitrary", "arbitrary"),
            vmem_limit_bytes=VMEM_LIMIT_BYTES),
        name="layer",
    )(x, mod, g_pre.reshape(1, d), w_in.astype(BF16), w_pool.astype(BF16),
      pool_scale.reshape(1, POOL_WIDTH), w_out.astype(BF16), g_post.reshape(1, d),
      _suffix_sum_matrix_t(ATT_TILE))
```

```python
import jax
import jax.numpy as jnp
import numpy as np
from jax import lax
from jax.experimental import pallas as pl
from jax.experimental.pallas import tpu as pltpu

F32 = jnp.float32
BF16 = jnp.bfloat16

EPS = 1e-6
POOL_WINDOWS = (2, 4, 8, 16)
POOL_GROUP = 128
HEAD_DIM = 64
N_HEADS = 8
N_PAIRS = N_HEADS // 2
PAIR_WIDTH = 2 * HEAD_DIM
SB_WIDTH = N_HEADS * HEAD_DIM
POOL_WIDTH = len(POOL_WINDOWS) * POOL_GROUP
MAX_WINDOW = max(POOL_WINDOWS)
U_COL, POOL_GATE_COL = 0, POOL_WIDTH
Q_COL, K_COL, V_COL, SB_GATE_COL = (2 * POOL_WIDTH + j * SB_WIDTH for j in range(4))

TOKEN_TILE = 256
TILES_PER_STEP = 4
ATT_TILE = 128
EXTRA_KEYS = 16
MXU_COLS = 256
VMEM_LIMIT_BYTES = 56 * 1024 * 1024

SKIP_BELOW = -89.0
MASKED_SCORE = -1e30


def _sigmoid(x):
    return 1.0 / (1.0 + jnp.exp(-x))


def _ada_kernel(c_ref, w_ref, b_ref, o_ref):
    c = c_ref[...]
    o_ref[0] = jnp.dot(c * _sigmoid(c), w_ref[...], preferred_element_type=F32,
                       precision=lax.Precision.HIGHEST) + b_ref[0]


def _log_terms(z):
    neg_part = jnp.minimum(z, 0.0)
    neg_pos_part = neg_part - z
    log1p_term = jnp.log(1.0 + jnp.exp(neg_part + neg_pos_part))
    return neg_part - log1p_term, neg_pos_part - log1p_term


def _hi_lo(x):
    hi = x.astype(BF16)
    lo = (x - hi.astype(F32)).astype(BF16)
    return jnp.concatenate([hi, lo], axis=0)


def _token_tile(i, rows, x_ref, mod_ref, gpre_ref, win_ref, wpool_ref, pscale_ref, wout_ref,
                gpost_ref, ucum_ref, o_ref, k_s, vt_s, qbd_s, acc_s, uprev_s, y_s, gate_s):
    tq = rows.stop - rows.start
    nsub = tq // ATT_TILE

    x = x_ref[0, rows, :]
    batch = pl.ds(pl.program_id(0), 1)
    shift, scale, gate = mod_ref[0, batch, :], mod_ref[1, batch, :], mod_ref[2, batch, :]
    hn = x * lax.rsqrt(jnp.mean(x * x, axis=-1, keepdims=True) + EPS)
    hb = (hn * (gpre_ref[...] * (1.0 + scale)) + shift).astype(BF16)

    def proj(first_col, width):
        return jnp.dot(hb, win_ref[:, first_col:first_col + width], preferred_element_type=F32)


    qt = (proj(Q_COL, SB_WIDTH) * (1.0 / np.sqrt(HEAD_DIM))).T.astype(BF16)
    k = proj(K_COL, SB_WIDTH).astype(BF16)
    first_head = lax.broadcasted_iota(jnp.int32, (PAIR_WIDTH, ATT_TILE), 0) < HEAD_DIM
    zero = jnp.zeros((PAIR_WIDTH, ATT_TILE), BF16)
    for s in range(nsub):
        cols = slice(s * ATT_TILE, (s + 1) * ATT_TILE)
        k_s[i * nsub + s + 1] = k[cols, :]
        for p in range(N_PAIRS):
            qp = qt[p * PAIR_WIDTH:(p + 1) * PAIR_WIDTH, cols]
            qbd_s[p, s] = jnp.concatenate(
                [jnp.where(first_head, qp, zero), jnp.where(first_head, zero, qp)], axis=1)

    def values_chunk(half):
        def emit():
            rows = slice(half * MXU_COLS, (half + 1) * MXU_COLS)
            vt = proj(V_COL + half * MXU_COLS, MXU_COLS).T.astype(BF16)
            for s in range(nsub):
                vt_s[i * nsub + s + 1, rows, :] = vt[:, s * ATT_TILE:(s + 1) * ATT_TILE]
        return emit

    pooled_out = [None] * len(POOL_WINDOWS)
    u_half = [None] * (POOL_WIDTH // MXU_COLS)

    def pool_in_chunk(half):
        def emit():
            u_half[half] = proj(U_COL + half * MXU_COLS, MXU_COLS)
        return emit

    def pool_chunk(half):
        def emit():
            cols = slice(half * MXU_COLS, (half + 1) * MXU_COLS)
            u = u_half[half]
            uext = jnp.concatenate([uprev_s[:, cols], u], axis=0)
            uprev_s[:, cols] = u[tq - MAX_WINDOW:, :]
            pos = i * tq + lax.broadcasted_iota(jnp.int32, (tq, 1), 0)
            for gg in range(MXU_COLS // POOL_GROUP):
                g = half * (MXU_COLS // POOL_GROUP) + gg
                w = POOL_WINDOWS[g]
                ug = uext[:, gg * POOL_GROUP:(gg + 1) * POOL_GROUP]
                acc, width = ug, 1
                while width < w:
                    acc = acc + pltpu.roll(acc, width, axis=0)
                    width *= 2
                count = jnp.minimum(pos + 1, w).astype(F32)
                pooled = acc[MAX_WINDOW:, :] / count - ug[MAX_WINDOW:, :]
                pooled_out[g] = jnp.dot(pooled.astype(BF16), wpool_ref[g],
                                        preferred_element_type=F32)
        return emit

    def gate_chunk(which, half):
        def emit():
            cols = slice(half * MXU_COLS, (half + 1) * MXU_COLS)
            g = proj((POOL_GATE_COL, SB_GATE_COL)[which] + half * MXU_COLS, MXU_COLS)
            gate_s[which, :, cols] = g * _sigmoid(g)
        return emit

    gated = []

    def pool_out_chunk(j):
        def emit():
            if not gated:
                y_pool = jnp.concatenate(pooled_out, axis=-1) * pscale_ref[...]
                gated.append((y_pool * gate_s[0]).astype(BF16))
            cols = slice(j * MXU_COLS, (j + 1) * MXU_COLS)
            y_s[:, cols] = jnp.dot(gated[0], wout_ref[:POOL_WIDTH, cols],
                                   preferred_element_type=F32)
        return emit

    ucum_t = ucum_ref[...]
    key_idx = lax.broadcasted_iota(jnp.int32, (ATT_TILE, 2 * ATT_TILE), 0)
    query_idx = lax.broadcasted_iota(jnp.int32, (ATT_TILE, 2 * ATT_TILE), 1) % ATT_TILE
    causal = key_idx < query_idx
    last_row = lax.broadcasted_iota(jnp.int32, (8, 2 * ATT_TILE), 0) == 7
    extra_sums_t = jnp.where(
        lax.broadcasted_iota(jnp.int32, (EXTRA_KEYS, 2 * EXTRA_KEYS), 1) % EXTRA_KEYS
        > lax.broadcasted_iota(jnp.int32, (EXTRA_KEYS, 2 * EXTRA_KEYS), 0), 1.0, 0.0).astype(BF16)
    chains = [(s, p) for s in range(nsub) for p in range(N_PAIRS)]
    lanes = [slice(p * PAIR_WIDTH, (p + 1) * PAIR_WIDTH) for p in range(N_PAIRS)]

    def slots_of(n):
        return [jnp.maximum(i * nsub + s - n + 1, 0) for s in range(nsub)]

    def run(hooks, c):
        if c < len(hooks):
            hooks[c]()


    def score_logs(n, mask=None, newest=False, without_newest=None):
        keys = slice(ATT_TILE - EXTRA_KEYS, ATT_TILE) if newest else slice(0, ATT_TILE)
        slots = slots_of(n)
        z = [jnp.dot(k_s[slots[s], keys, lanes[p]], qbd_s[p, s], preferred_element_type=F32)
             for s, p in chains]
        logs = []
        for zc in z:
            if mask is not None:
                zc = jnp.where(mask, zc, MASKED_SCORE)
            if without_newest is not None:
                zc = jnp.concatenate(
                    [zc[:-EXTRA_KEYS],
                     jnp.where(without_newest, MASKED_SCORE, zc[-EXTRA_KEYS:])], axis=0)
            logs.append(_log_terms(zc))
        return logs

    def suffix_sums(logs, carries, newest=False, after=()):
        sums_t = extra_sums_t if newest else ucum_t
        tails, new_carries = [], []
        for c, (_, log_1mb) in enumerate(logs):
            if carries is None:
                sums = jnp.dot(sums_t, _hi_lo(log_1mb), preferred_element_type=F32)
            else:
                on_last = jnp.where(last_row, carries[c], 0.0)
                with_carry = jnp.concatenate([log_1mb[:-8], log_1mb[-8:] + on_last], axis=0)
                sums = jnp.dot(sums_t, _hi_lo(with_carry), preferred_element_type=F32)
                sums = jnp.concatenate([sums[:-8], sums[-8:] + on_last], axis=0)
            tails.append(sums)
            new_carries.append(sums[0:1, :] + log_1mb[0:1, :])
            run(after, c)
        return tails, tuple(new_carries)

    def weights(logs, tails, c, newest=False):
        a = jnp.exp(logs[c][0] + tails[c]).astype(BF16)
        if newest:
            a = jnp.concatenate(
                [jnp.zeros((ATT_TILE - EXTRA_KEYS, 2 * ATT_TILE), BF16), a], axis=0)
        return a

    def weighted_values(steps, a, c, first):
        s, p = chains[c]
        v_t = jnp.concatenate([vt_s[slots_of(n)[s], lanes[p], :] for n in steps], axis=1)
        o = jnp.dot(v_t, jnp.concatenate(a, axis=0), preferred_element_type=F32)
        o = jnp.concatenate([o[:HEAD_DIM, :ATT_TILE], o[HEAD_DIM:, ATT_TILE:]], axis=0)
        if first:
            acc_s[p, s] = o
        else:
            acc_s[p, s] += o

    def needed(n, carries):
        go = None
        for s in range(nsub):
            live = carries[s * N_PAIRS]
            for p in range(1, N_PAIRS):
                live = jnp.maximum(live, carries[s * N_PAIRS + p])
            go_s = jnp.logical_and(i * nsub + s - n >= 0, jnp.max(live) >= SKIP_BELOW)
            go = go_s if go is None else jnp.logical_or(go, go_s)
        return go

    skip = lambda: None
    pool_in_chunk(0)()
    logs0 = score_logs(0, mask=causal)
    values_chunk(0)()
    tails0, carries = suffix_sums(
        logs0, None, after=[values_chunk(1), skip, pool_in_chunk(1), skip])
    a0 = []
    gate_hooks = [skip, gate_chunk(0, 0), skip, gate_chunk(0, 1)]
    for c in range(len(chains)):
        a0.append(weights(logs0, tails0, c))
        run(gate_hooks, c)
    logs1 = score_logs(1)
    logs2 = score_logs(2, newest=True)
    gate_chunk(1, 0)()
    tails1, carries = suffix_sums(
        logs1, carries, after=[gate_chunk(1, 1), skip, pool_chunk(0), skip,
                               pool_chunk(1), skip, pool_out_chunk(0), pool_out_chunk(1)])
    tails2, carries = suffix_sums(
        logs2, carries, newest=True,
        after=[pool_out_chunk(j) for j in range(2, wout_ref.shape[1] // MXU_COLS)])
    for c in range(len(chains)):
        weighted_values((0, 1, 2), [a0[c], weights(logs1, tails1, c),
                                    weights(logs2, tails2, c, newest=True)], c, first=True)

    def loop_body(state):
        n, _, carries = state
        logs = score_logs(n, without_newest=(n == 2))
        tails, carries = suffix_sums(logs, carries)
        for c in range(len(chains)):
            weighted_values((n,), [weights(logs, tails, c)], c, first=False)
        return n + 1, needed(n + 1, carries), carries

    def finish():
        att_t = jnp.concatenate(
            [jnp.concatenate([acc_s[p, s] for s in range(nsub)], axis=1)
             for p in range(N_PAIRS)], axis=0)
        y_sb = att_t.T * gate_s[1]
        y = y_s[...] + jnp.dot(y_sb.astype(BF16), wout_ref[POOL_WIDTH:, :],
                               preferred_element_type=F32)
        yn = y * lax.rsqrt(jnp.mean(y * y, axis=-1, keepdims=True) + EPS)
        o_ref[0, rows, :] = x_ref[0, rows, :] + yn * (gate * gpost_ref[...])

    finish()

    def further_steps():
        lax.while_loop(lambda state: state[1], loop_body, (jnp.int32(2), jnp.bool_(True), carries))
        finish()

    return needed(2, carries), further_steps


def _layer_kernel(x_ref, mod_ref, gpre_ref, win_ref, wpool_ref, pscale_ref, wout_ref,
                  gpost_ref, ucum_ref, o_ref, k_s, vt_s, qbd_s, acc_s, uprev_s, y_s, gate_s):
    step = pl.program_id(1)

    @pl.when(step == 0)
    def _():
        uprev_s[...] = jnp.zeros_like(uprev_s)
        k_s[0] = jnp.zeros_like(k_s[0])
        vt_s[0] = jnp.zeros_like(vt_s[0])

    unfinished = [
        _token_tile(step * TILES_PER_STEP + t, slice(t * TOKEN_TILE, (t + 1) * TOKEN_TILE),
                    x_ref, mod_ref, gpre_ref, win_ref, wpool_ref, pscale_ref, wout_ref,
                    gpost_ref, ucum_ref, o_ref, k_s, vt_s, qbd_s.at[t], acc_s.at[t], uprev_s,
                    y_s.at[t], gate_s.at[t])
        for t in range(TILES_PER_STEP)]
    for more_needed, further_steps in unfinished:
        pl.when(more_needed)(further_steps)


def _suffix_sum_matrix_t(k):
    later = np.arange(k)[None, :] > np.arange(k)[:, None]
    return jnp.asarray(np.concatenate([later, later], axis=1).astype(np.float32), dtype=BF16)


@jax.jit
def kernel(x, c, w_ada, b_ada, g_pre, w_in, w_pool, pool_scale, w_out, g_post):
    b, s, d = x.shape
    tq = TOKEN_TILE
    nsub = tq // ATT_TILE
    n_key_tiles = s // ATT_TILE
    step_rows = tq * TILES_PER_STEP
    assert s % step_rows == 0 and tq % ATT_TILE == 0 and tq >= MAX_WINDOW
    assert w_in.shape == (d, 2 * POOL_WIDTH + 4 * SB_WIDTH) and w_out.shape == (d, d)

    mod = pl.pallas_call(
        _ada_kernel,
        grid=(3,),
        in_specs=[pl.BlockSpec((b, d), lambda n: (0, 0)),
                  pl.BlockSpec((d, d), lambda n: (0, n)),
                  pl.BlockSpec((1, 1, d), lambda n: (n, 0, 0))],
        out_specs=pl.BlockSpec((1, b, d), lambda n: (n, 0, 0)),
        out_shape=jax.ShapeDtypeStruct((3, b, d), F32),
        name="adaln",
    )(c, w_ada, b_ada.reshape(3, 1, d))

    const = lambda *shape: pl.BlockSpec(shape, lambda bi, ti: (0,) * len(shape))
    return pl.pallas_call(
        _layer_kernel,
        grid=(b, s // step_rows),
        in_specs=[pl.BlockSpec((1, step_rows, d), lambda bi, ti: (bi, ti, 0)),
                  const(3, b, d),
                  const(1, d),
                  const(d, w_in.shape[1]),
                  const(len(POOL_WINDOWS), POOL_GROUP, POOL_GROUP),
                  const(1, POOL_WIDTH),
                  const(d, d),
                  const(1, d),
                  const(ATT_TILE, 2 * ATT_TILE)],
        out_specs=pl.BlockSpec((1, step_rows, d), lambda bi, ti: (bi, ti, 0)),
        out_shape=jax.ShapeDtypeStruct((b, s, d), F32),
        scratch_shapes=[
            pltpu.VMEM((n_key_tiles + 1, ATT_TILE, SB_WIDTH), BF16),
            pltpu.VMEM((n_key_tiles + 1, SB_WIDTH, ATT_TILE), BF16),
            pltpu.VMEM((TILES_PER_STEP, N_PAIRS, nsub, PAIR_WIDTH, 2 * ATT_TILE), BF16),
            pltpu.VMEM((TILES_PER_STEP, N_PAIRS, nsub, PAIR_WIDTH, ATT_TILE), F32),
            pltpu.VMEM((MAX_WINDOW, POOL_WIDTH), F32),
            pltpu.VMEM((TILES_PER_STEP, tq, d), F32),
            pltpu.VMEM((TILES_PER_STEP, 2, tq, POOL_WIDTH), F32)],
        compiler_params=pltpu.CompilerParams(
            dimension_semantics=("arbitrary", "arbitrary"),
            vmem_limit_bytes=VMEM_LIMIT_BYTES),
        name="layer",
    )(x, mod, g_pre.reshape(1, d), w_in.astype(BF16), w_pool.astype(BF16),
      pool_scale.reshape(1, POOL_WIDTH), w_out.astype(BF16), g_post.reshape(1, d),
      _suffix_sum_matrix_t(ATT_TILE))
```

```python
import jax
import jax.numpy as jnp
import numpy as np
from jax import lax
from jax.experimental import pallas as pl
from jax.experimental.pallas import tpu as pltpu

F32 = jnp.float32
BF16 = jnp.bfloat16

EPS = 1e-6
POOL_WINDOWS = (2, 4, 8, 16)
POOL_GROUP = 128
HEAD_DIM = 64
N_HEADS = 8
N_PAIRS = N_HEADS // 2
PAIR_WIDTH = 2 * HEAD_DIM
SB_WIDTH = N_HEADS * HEAD_DIM
POOL_WIDTH = len(POOL_WINDOWS) * POOL_GROUP
MAX_WINDOW = max(POOL_WINDOWS)
U_COL, POOL_GATE_COL = 0, POOL_WIDTH
Q_COL, K_COL, V_COL, SB_GATE_COL = (2 * POOL_WIDTH + j * SB_WIDTH for j in range(4))

TOKEN_TILE = 256
TILES_PER_STEP = 4
ATT_TILE = 128
EXTRA_KEYS = 16
MXU_COLS = 256
VMEM_LIMIT_BYTES = 56 * 1024 * 1024

SKIP_BELOW = -89.0
MASKED_SCORE = -1e30


def _sigmoid(x):
    return 1.0 / (1.0 + jnp.exp(-x))


def _ada_kernel(c_ref, w_ref, b_ref, o_ref):
    c = c_ref[...]
    o_ref[0] = jnp.dot(c * _sigmoid(c), w_ref[...], preferred_element_type=F32,
                       precision=lax.Precision.HIGHEST) + b_ref[0]


def _log_terms(z):
    neg_part = jnp.minimum(z, 0.0)
    neg_pos_part = neg_part - z
    log1p_term = jnp.log(1.0 + jnp.exp(neg_part + neg_pos_part))
    return neg_part - log1p_term, neg_pos_part - log1p_term


def _hi_lo(x):
    hi = x.astype(BF16)
    lo = (x - hi.astype(F32)).astype(BF16)
    return jnp.concatenate([hi, lo], axis=0)


def _token_tile(i, rows, x_ref, mod_ref, gpre_ref, win_ref, wpool_ref, pscale_ref, wout_ref,
                gpost_ref, ucum_ref, o_ref, k_s, vt_s, qbd_s, acc_s, uprev_s, y_s, gate_s):
    tq = rows.stop - rows.start
    nsub = tq // ATT_TILE

    x = x_ref[0, rows, :]
    batch = pl.ds(pl.program_id(0), 1)
    shift, scale, gate = mod_ref[0, batch, :], mod_ref[1, batch, :], mod_ref[2, batch, :]
    hn = x * lax.rsqrt(jnp.mean(x * x, axis=-1, keepdims=True) + EPS)
    hb = (hn * (gpre_ref[...] * (1.0 + scale)) + shift).astype(BF16)

    def proj(first_col, width):
        return jnp.dot(hb, win_ref[:, first_col:first_col + width], preferred_element_type=F32)


    qt = (proj(Q_COL, SB_WIDTH) * (1.0 / np.sqrt(HEAD_DIM))).T.astype(BF16)
    k = proj(K_COL, SB_WIDTH).astype(BF16)
    first_head = lax.broadcasted_iota(jnp.int32, (PAIR_WIDTH, ATT_TILE), 0) < HEAD_DIM
    zero = jnp.zeros((PAIR_WIDTH, ATT_TILE), BF16)
    for s in range(nsub):
        cols = slice(s * ATT_TILE, (s + 1) * ATT_TILE)
        k_s[i * nsub + s + 1] = k[cols, :]
        for p in range(N_PAIRS):
            qp = qt[p * PAIR_WIDTH:(p + 1) * PAIR_WIDTH, cols]
            qbd_s[p, s] = jnp.concatenate(
                [jnp.where(first_head, qp, zero), jnp.where(first_head, zero, qp)], axis=1)

    def values_chunk(half):
        def emit():
            rows = slice(half * MXU_COLS, (half + 1) * MXU_COLS)
            vt = proj(V_COL + half * MXU_COLS, MXU_COLS).T.astype(BF16)
            for s in range(nsub):
                vt_s[i * nsub + s + 1, rows, :] = vt[:, s * ATT_TILE:(s + 1) * ATT_TILE]
        return emit

    pooled_out = [None] * len(POOL_WINDOWS)
    u_half = [None] * (POOL_WIDTH // MXU_COLS)

    def pool_in_chunk(half):
        def emit():
            u_half[half] = proj(U_COL + half * MXU_COLS, MXU_COLS)
        return emit

    def pool_chunk(half):
        def emit():
            cols = slice(half * MXU_COLS, (half + 1) * MXU_COLS)
            u = u_half[half]
            uext = jnp.concatenate([uprev_s[:, cols], u], axis=0)
            uprev_s[:, cols] = u[tq - MAX_WINDOW:, :]
            pos = i * tq + lax.broadcasted_iota(jnp.int32, (tq, 1), 0)
            for gg in range(MXU_COLS // POOL_GROUP):
                g = half * (MXU_COLS // POOL_GROUP) + gg
                w = POOL_WINDOWS[g]
                ug = uext[:, gg * POOL_GROUP:(gg + 1) * POOL_GROUP]
                acc, width = ug, 1
                while width < w:
                    acc = acc + pltpu.roll(acc, width, axis=0)
                    width *= 2
                count = jnp.minimum(pos + 1, w).astype(F32)
                pooled = acc[MAX_WINDOW:, :] / count - ug[MAX_WINDOW:, :]
                pooled_out[g] = jnp.dot(pooled.astype(BF16), wpool_ref[g],
                                        preferred_element_type=F32)
        return emit

    def gate_chunk(which, half):
        def emit():
            cols = slice(half * MXU_COLS, (half + 1) * MXU_COLS)
            g = proj((POOL_GATE_COL, SB_GATE_COL)[which] + half * MXU_COLS, MXU_COLS)
            gate_s[which, :, cols] = g * _sigmoid(g)
        return emit

    gated = []

    def pool_out_chunk(j):
        def emit():
            if not gated:
                y_pool = jnp.concatenate(pooled_out, axis=-1) * pscale_ref[...]
                gated.append((y_pool * gate_s[0]).astype(BF16))
            cols = slice(j * MXU_COLS, (j + 1) * MXU_COLS)
            y_s[:, cols] = jnp.dot(gated[0], wout_ref[:POOL_WIDTH, cols],
                                   preferred_element_type=F32)
        return emit

    ucum_t = ucum_ref[...]
    key_idx = lax.broadcasted_iota(jnp.int32, (ATT_TILE, 2 * ATT_TILE), 0)
    query_idx = lax.broadcasted_iota(jnp.int32, (ATT_TILE, 2 * ATT_TILE), 1) % ATT_TILE
    causal = key_idx < query_idx
    last_row = lax.broadcasted_iota(jnp.int32, (8, 2 * ATT_TILE), 0) == 7
    extra_sums_t = jnp.where(
        lax.broadcasted_iota(jnp.int32, (EXTRA_KEYS, 2 * EXTRA_KEYS), 1) % EXTRA_KEYS
        > lax.broadcasted_iota(jnp.int32, (EXTRA_KEYS, 2 * EXTRA_KEYS), 0), 1.0, 0.0).astype(BF16)
    chains = [(s, p) for s in range(nsub) for p in range(N_PAIRS)]
    lanes = [slice(p * PAIR_WIDTH, (p + 1) * PAIR_WIDTH) for p in range(N_PAIRS)]

    def slots_of(n):
        return [jnp.maximum(i * nsub + s - n + 1, 0) for s in range(nsub)]

    def run(hooks, c):
        if c < len(hooks):
            hooks[c]()


    def score_logs(n, mask=None, newest=False, without_newest=None):
        keys = slice(ATT_TILE - EXTRA_KEYS, ATT_TILE) if newest else slice(0, ATT_TILE)
        slots = slots_of(n)
        z = [jnp.dot(k_s[slots[s], keys, lanes[p]], qbd_s[p, s], preferred_element_type=F32)
             for s, p in chains]
        logs = []
        for zc in z:
            if mask is not None:
                zc = jnp.where(mask, zc, MASKED_SCORE)
            if without_newest is not None:
                zc = jnp.concatenate(
                    [zc[:-EXTRA_KEYS],
                     jnp.where(without_newest, MASKED_SCORE, zc[-EXTRA_KEYS:])], axis=0)
            logs.append(_log_terms(zc))
        return logs

    def suffix_sums(logs, carries, newest=False, after=()):
        sums_t = extra_sums_t if newest else ucum_t
        tails, new_carries = [], []
        for c, (_, log_1mb) in enumerate(logs):
            if carries is None:
                sums = jnp.dot(sums_t, _hi_lo(log_1mb), preferred_element_type=F32)
            else:
                on_last = jnp.where(last_row, carries[c], 0.0)
                with_carry = jnp.concatenate([log_1mb[:-8], log_1mb[-8:] + on_last], axis=0)
                sums = jnp.dot(sums_t, _hi_lo(with_carry), preferred_element_type=F32)
                sums = jnp.concatenate([sums[:-8], sums[-8:] + on_last], axis=0)
            tails.append(sums)
            new_carries.append(sums[0:1, :] + log_1mb[0:1, :])
            run(after, c)
        return tails, tuple(new_carries)

    def weights(logs, tails, c, newest=False):
        a = jnp.exp(logs[c][0] + tails[c]).astype(BF16)
        if newest:
            a = jnp.concatenate(
                [jnp.zeros((ATT_TILE - EXTRA_KEYS, 2 * ATT_TILE), BF16), a], axis=0)
        return a

    def weighted_values(steps, a, c, first):
        s, p = chains[c]
        v_t = jnp.concatenate([vt_s[slots_of(n)[s], lanes[p], :] for n in steps], axis=1)
        o = jnp.dot(v_t, jnp.concatenate(a, axis=0), preferred_element_type=F32)
        o = jnp.concatenate([o[:HEAD_DIM, :ATT_TILE], o[HEAD_DIM:, ATT_TILE:]], axis=0)
        if first:
            acc_s[p, s] = o
        else:
            acc_s[p, s] += o

    def needed(n, carries):
        go = None
        for s in range(nsub):
            live = carries[s * N_PAIRS]
            for p in range(1, N_PAIRS):
                live = jnp.maximum(live, carries[s * N_PAIRS + p])
            go_s = jnp.logical_and(i * nsub + s - n >= 0, jnp.max(live) >= SKIP_BELOW)
            go = go_s if go is None else jnp.logical_or(go, go_s)
        return go

    skip = lambda: None
    pool_in_chunk(0)()
    logs0 = score_logs(0, mask=causal)
    values_chunk(0)()
    tails0, carries = suffix_sums(
        logs0, None, after=[values_chunk(1), skip, pool_in_chunk(1), skip])
    a0 = []
    gate_hooks = [skip, gate_chunk(0, 0), skip, gate_chunk(0, 1)]
    for c in range(len(chains)):
        a0.append(weights(logs0, tails0, c))
        run(gate_hooks, c)
    logs1 = score_logs(1)
    logs2 = score_logs(2, newest=True)
    gate_chunk(1, 0)()
    tails1, carries = suffix_sums(
        logs1, carries, after=[gate_chunk(1, 1)])
    tails2, carries = suffix_sums(logs2, carries, newest=True)
    out_hooks = [pool_chunk(0), pool_chunk(1)] + [
        pool_out_chunk(j) for j in range(wout_ref.shape[1] // MXU_COLS)]
    for c in range(len(chains)):
        weighted_values((0, 1, 2), [a0[c], weights(logs1, tails1, c),
                                    weights(logs2, tails2, c, newest=True)], c, first=True)
        run(out_hooks, c)

    def loop_body(state):
        n, _, carries = state
        logs = score_logs(n, without_newest=(n == 2))
        tails, carries = suffix_sums(logs, carries)
        for c in range(len(chains)):
            weighted_values((n,), [weights(logs, tails, c)], c, first=False)
        return n + 1, needed(n + 1, carries), carries

    def finish():
        att_t = jnp.concatenate(
            [jnp.concatenate([acc_s[p, s] for s in range(nsub)], axis=1)
             for p in range(N_PAIRS)], axis=0)
        y_sb = att_t.T * gate_s[1]
        y = y_s[...] + jnp.dot(y_sb.astype(BF16), wout_ref[POOL_WIDTH:, :],
                               preferred_element_type=F32)
        yn = y * lax.rsqrt(jnp.mean(y * y, axis=-1, keepdims=True) + EPS)
        o_ref[0, rows, :] = x_ref[0, rows, :] + yn * (gate * gpost_ref[...])

    finish()

    def further_steps():
        lax.while_loop(lambda state: state[1], loop_body, (jnp.int32(2), jnp.bool_(True), carries))
        finish()

    return needed(2, carries), further_steps


def _layer_kernel(x_ref, mod_ref, gpre_ref, win_ref, wpool_ref, pscale_ref, wout_ref,
                  gpost_ref, ucum_ref, o_ref, k_s, vt_s, qbd_s, acc_s, uprev_s, y_s, gate_s):
    step = pl.program_id(1)

    @pl.when(step == 0)
    def _():
        uprev_s[...] = jnp.zeros_like(uprev_s)
        k_s[0] = jnp.zeros_like(k_s[0])
        vt_s[0] = jnp.zeros_like(vt_s[0])

    unfinished = [
        _token_tile(step * TILES_PER_STEP + t, slice(t * TOKEN_TILE, (t + 1) * TOKEN_TILE),
                    x_ref, mod_ref, gpre_ref, win_ref, wpool_ref, pscale_ref, wout_ref,
                    gpost_ref, ucum_ref, o_ref, k_s, vt_s, qbd_s.at[t], acc_s.at[t], uprev_s,
                    y_s.at[t], gate_s.at[t])
        for t in range(TILES_PER_STEP)]
    for more_needed, further_steps in unfinished:
        pl.when(more_needed)(further_steps)


def _suffix_sum_matrix_t(k):
    later = np.arange(k)[None, :] > np.arange(k)[:, None]
    return jnp.asarray(np.concatenate([later, later], axis=1).astype(np.float32), dtype=BF16)


@jax.jit
def kernel(x, c, w_ada, b_ada, g_pre, w_in, w_pool, pool_scale, w_out, g_post):
    b, s, d = x.shape
    tq = TOKEN_TILE
    nsub = tq // ATT_TILE
    n_key_tiles = s // ATT_TILE
    step_rows = tq * TILES_PER_STEP
    assert s % step_rows == 0 and tq % ATT_TILE == 0 and tq >= MAX_WINDOW
    assert w_in.shape == (d, 2 * POOL_WIDTH + 4 * SB_WIDTH) and w_out.shape == (d, d)

    mod = pl.pallas_call(
        _ada_kernel,
        grid=(3,),
        in_specs=[pl.BlockSpec((b, d), lambda n: (0, 0)),
                  pl.BlockSpec((d, d), lambda n: (0, n)),
                  pl.BlockSpec((1, 1, d), lambda n: (n, 0, 0))],
        out_specs=pl.BlockSpec((1, b, d), lambda n: (n, 0, 0)),
        out_shape=jax.ShapeDtypeStruct((3, b, d), F32),
        name="adaln",
    )(c, w_ada, b_ada.reshape(3, 1, d))

    const = lambda *shape: pl.BlockSpec(shape, lambda bi, ti: (0,) * len(shape))
    return pl.pallas_call(
        _layer_kernel,
        grid=(b, s // step_rows),
        in_specs=[pl.BlockSpec((1, step_rows, d), lambda bi, ti: (bi, ti, 0)),
                  const(3, b, d),
                  const(1, d),
                  const(d, w_in.shape[1]),
                  const(len(POOL_WINDOWS), POOL_GROUP, POOL_GROUP),
                  const(1, POOL_WIDTH),
                  const(d, d),
                  const(1, d),
                  const(ATT_TILE, 2 * ATT_TILE)],
        out_specs=pl.BlockSpec((1, step_rows, d), lambda bi, ti: (bi, ti, 0)),
        out_shape=jax.ShapeDtypeStruct((b, s, d), F32),
        scratch_shapes=[
            pltpu.VMEM((n_key_tiles + 1, ATT_TILE, SB_WIDTH), BF16),
            pltpu.VMEM((n_key_tiles + 1, SB_WIDTH, ATT_TILE), BF16),
            pltpu.VMEM((TILES_PER_STEP, N_PAIRS, nsub, PAIR_WIDTH, 2 * ATT_TILE), BF16),
            pltpu.VMEM((TILES_PER_STEP, N_PAIRS, nsub, PAIR_WIDTH, ATT_TILE), F32),
            pltpu.VMEM((MAX_WINDOW, POOL_WIDTH), F32),
            pltpu.VMEM((TILES_PER_STEP, tq, d), F32),
            pltpu.VMEM((TILES_PER_STEP, 2, tq, POOL_WIDTH), F32)],
        compiler_params=pltpu.CompilerParams(
            dimension_semantics=("arbitrary", "arbitrary"),
            vmem_limit_bytes=VMEM_LIMIT_BYTES),
        name="layer",
    )(x, mod, g_pre.reshape(1, d), w_in.astype(BF16), w_pool.astype(BF16),
      pool_scale.reshape(1, POOL_WIDTH), w_out.astype(BF16), g_post.reshape(1, d),
      _suffix_sum_matrix_t(ATT_TILE))
```

```python
import jax
import jax.numpy as jnp
import numpy as np
from jax import lax
from jax.experimental import pallas as pl
from jax.experimental.pallas import tpu as pltpu

F32 = jnp.float32
BF16 = jnp.bfloat16

EPS = 1e-6
POOL_WINDOWS = (2, 4, 8, 16)
POOL_GROUP = 128
HEAD_DIM = 64
N_HEADS = 8
N_PAIRS = N_HEADS // 2
PAIR_WIDTH = 2 * HEAD_DIM
SB_WIDTH = N_HEADS * HEAD_DIM
POOL_WIDTH = len(POOL_WINDOWS) * POOL_GROUP
MAX_WINDOW = max(POOL_WINDOWS)
U_COL, POOL_GATE_COL = 0, POOL_WIDTH
Q_COL, K_COL, V_COL, SB_GATE_COL = (2 * POOL_WIDTH + j * SB_WIDTH for j in range(4))

TOKEN_TILE = 256
TILES_PER_STEP = 4
ATT_TILE = 128
EXTRA_KEYS = 16
MXU_COLS = 256
VMEM_LIMIT_BYTES = 56 * 1024 * 1024

SKIP_BELOW = -89.0
MASKED_SCORE = -1e30


def _sigmoid(x):
    return 1.0 / (1.0 + jnp.exp(-x))


def _ada_kernel(c_ref, w_ref, b_ref, o_ref):
    c = c_ref[...]
    o_ref[0] = jnp.dot(c * _sigmoid(c), w_ref[...], preferred_element_type=F32,
                       precision=lax.Precision.HIGHEST) + b_ref[0]


def _log_terms(z):
    neg_part = jnp.minimum(z, 0.0)
    neg_pos_part = neg_part - z
    log1p_term = jnp.log(1.0 + jnp.exp(neg_part + neg_pos_part))
    return neg_part - log1p_term, neg_pos_part - log1p_term


def _hi_lo(x):
    hi = x.astype(BF16)
    lo = (x - hi.astype(F32)).astype(BF16)
    return jnp.concatenate([hi, lo], axis=0)


def _token_tile(i, rows, x_ref, mod_ref, gpre_ref, win_ref, wpool_ref, pscale_ref, wout_ref,
                gpost_ref, ucum_ref, o_ref, k_s, vt_s, qbd_s, acc_s, uprev_s, y_s, gate_s):
    tq = rows.stop - rows.start
    nsub = tq // ATT_TILE

    x = x_ref[0, rows, :]
    batch = pl.ds(pl.program_id(0), 1)
    shift, scale, gate = mod_ref[0, batch, :], mod_ref[1, batch, :], mod_ref[2, batch, :]
    hn = x * lax.rsqrt(jnp.mean(x * x, axis=-1, keepdims=True) + EPS)
    hb = (hn * (gpre_ref[...] * (1.0 + scale)) + shift).astype(BF16)

    def proj(first_col, width):
        return jnp.dot(hb, win_ref[:, first_col:first_col + width], preferred_element_type=F32)


    qt = (proj(Q_COL, SB_WIDTH) * (1.0 / np.sqrt(HEAD_DIM))).T.astype(BF16)
    k = proj(K_COL, SB_WIDTH).astype(BF16)
    first_head = lax.broadcasted_iota(jnp.int32, (PAIR_WIDTH, ATT_TILE), 0) < HEAD_DIM
    zero = jnp.zeros((PAIR_WIDTH, ATT_TILE), BF16)
    for s in range(nsub):
        cols = slice(s * ATT_TILE, (s + 1) * ATT_TILE)
        k_s[i * nsub + s + 1] = k[cols, :]
        for p in range(N_PAIRS):
            qp = qt[p * PAIR_WIDTH:(p + 1) * PAIR_WIDTH, cols]
            qbd_s[p, s] = jnp.concatenate(
                [jnp.where(first_head, qp, zero), jnp.where(first_head, zero, qp)], axis=1)

    def values_chunk(half):
        def emit():
            rows = slice(half * MXU_COLS, (half + 1) * MXU_COLS)
            vt = proj(V_COL + half * MXU_COLS, MXU_COLS).T.astype(BF16)
            for s in range(nsub):
                vt_s[i * nsub + s + 1, rows, :] = vt[:, s * ATT_TILE:(s + 1) * ATT_TILE]
        return emit

    pooled_out = [None] * len(POOL_WINDOWS)
    u_half = [None] * (POOL_WIDTH // MXU_COLS)

    def pool_in_chunk(half):
        def emit():
            u_half[half] = proj(U_COL + half * MXU_COLS, MXU_COLS)
        return emit

    def pool_chunk(half):
        def emit():
            cols = slice(half * MXU_COLS, (half + 1) * MXU_COLS)
            u = u_half[half]
            uext = jnp.concatenate([uprev_s[:, cols], u], axis=0)
            uprev_s[:, cols] = u[tq - MAX_WINDOW:, :]
            pos = i * tq + lax.broadcasted_iota(jnp.int32, (tq, 1), 0)
            for gg in range(MXU_COLS // POOL_GROUP):
                g = half * (MXU_COLS // POOL_GROUP) + gg
                w = POOL_WINDOWS[g]
                ug = uext[:, gg * POOL_GROUP:(gg + 1) * POOL_GROUP]
                acc, width = ug, 1
                while width < w:
                    acc = acc + pltpu.roll(acc, width, axis=0)
                    width *= 2
                count = jnp.minimum(pos + 1, w).astype(F32)
                pooled = acc[MAX_WINDOW:, :] / count - ug[MAX_WINDOW:, :]
                pooled_out[g] = jnp.dot(pooled.astype(BF16), wpool_ref[g],
                                        preferred_element_type=F32)
        return emit

    def gate_chunk(which, half):
        def emit():
            cols = slice(half * MXU_COLS, (half + 1) * MXU_COLS)
            g = proj((POOL_GATE_COL, SB_GATE_COL)[which] + half * MXU_COLS, MXU_COLS)
            gate_s[which, :, cols] = g * _sigmoid(g)
        return emit

    gated = []

    def pool_out_chunk(j):
        def emit():
            if not gated:
                y_pool = jnp.concatenate(pooled_out, axis=-1) * pscale_ref[...]
                gated.append((y_pool * gate_s[0]).astype(BF16))
            cols = slice(j * MXU_COLS, (j + 1) * MXU_COLS)
            y_s[:, cols] = jnp.dot(gated[0], wout_ref[:POOL_WIDTH, cols],
                                   preferred_element_type=F32)
        return emit

    ucum_t = ucum_ref[...]
    key_idx = lax.broadcasted_iota(jnp.int32, (ATT_TILE, 2 * ATT_TILE), 0)
    query_idx = lax.broadcasted_iota(jnp.int32, (ATT_TILE, 2 * ATT_TILE), 1) % ATT_TILE
    causal = key_idx < query_idx
    last_row = lax.broadcasted_iota(jnp.int32, (8, 2 * ATT_TILE), 0) == 7
    extra_sums_t = jnp.where(
        lax.broadcasted_iota(jnp.int32, (EXTRA_KEYS, 2 * EXTRA_KEYS), 1) % EXTRA_KEYS
        > lax.broadcasted_iota(jnp.int32, (EXTRA_KEYS, 2 * EXTRA_KEYS), 0), 1.0, 0.0).astype(BF16)
    chains = [(s, p) for s in range(nsub) for p in range(N_PAIRS)]
    lanes = [slice(p * PAIR_WIDTH, (p + 1) * PAIR_WIDTH) for p in range(N_PAIRS)]

    def slots_of(n):
        return [jnp.maximum(i * nsub + s - n + 1, 0) for s in range(nsub)]

    def run(hooks, c):
        if c < len(hooks):
            hooks[c]()


    def score_logs(n, mask=None, newest=False, without_newest=None):
        keys = slice(ATT_TILE - EXTRA_KEYS, ATT_TILE) if newest else slice(0, ATT_TILE)
        slots = slots_of(n)
        z = [jnp.dot(k_s[slots[s], keys, lanes[p]], qbd_s[p, s], preferred_element_type=F32)
             for s, p in chains]
        logs = []
        for zc in z:
            if mask is not None:
                zc = jnp.where(mask, zc, MASKED_SCORE)
            if without_newest is not None:
                zc = jnp.concatenate(
                    [zc[:-EXTRA_KEYS],
                     jnp.where(without_newest, MASKED_SCORE, zc[-EXTRA_KEYS:])], axis=0)
            logs.append(_log_terms(zc))
        return logs

    def suffix_sums(logs, carries, newest=False, after=()):
        sums_t = extra_sums_t if newest else ucum_t
        tails, new_carries = [], []
        for c, (_, log_1mb) in enumerate(logs):
            if carries is None:
                sums = jnp.dot(sums_t, _hi_lo(log_1mb), preferred_element_type=F32)
            else:
                on_last = jnp.where(last_row, carries[c], 0.0)
                with_carry = jnp.concatenate([log_1mb[:-8], log_1mb[-8:] + on_last], axis=0)
                sums = jnp.dot(sums_t, _hi_lo(with_carry), preferred_element_type=F32)
                sums = jnp.concatenate([sums[:-8], sums[-8:] + on_last], axis=0)
            tails.append(sums)
            new_carries.append(sums[0:1, :] + log_1mb[0:1, :])
            run(after, c)
        return tails, tuple(new_carries)

    def weights(logs, tails, c, newest=False):
        a = jnp.exp(logs[c][0] + tails[c]).astype(BF16)
        if newest:
            a = jnp.concatenate(
                [jnp.zeros((ATT_TILE - EXTRA_KEYS, 2 * ATT_TILE), BF16), a], axis=0)
        return a

    def weighted_values(steps, a, c, first):
        s, p = chains[c]
        v_t = jnp.concatenate([vt_s[slots_of(n)[s], lanes[p], :] for n in steps], axis=1)
        o = jnp.dot(v_t, jnp.concatenate(a, axis=0), preferred_element_type=F32)
        o = jnp.concatenate([o[:HEAD_DIM, :ATT_TILE], o[HEAD_DIM:, ATT_TILE:]], axis=0)
        if first:
            acc_s[p, s] = o
        else:
            acc_s[p, s] += o

    def needed(n, carries):
        go = None
        for s in range(nsub):
            live = carries[s * N_PAIRS]
            for p in range(1, N_PAIRS):
                live = jnp.maximum(live, carries[s * N_PAIRS + p])
            go_s = jnp.logical_and(i * nsub + s - n >= 0, jnp.max(live) >= SKIP_BELOW)
            go = go_s if go is None else jnp.logical_or(go, go_s)
        return go

    skip = lambda: None
    pool_in_chunk(0)()
    logs0 = score_logs(0, mask=causal)
    values_chunk(0)()
    values_chunk(1)()
    tails0, carries = suffix_sums(logs0, None, after=[pool_in_chunk(1)])
    a0 = []
    gate_hooks = [skip, gate_chunk(0, 0), skip, gate_chunk(0, 1)]
    for c in range(len(chains)):
        a0.append(weights(logs0, tails0, c))
        run(gate_hooks, c)
    logs1 = score_logs(1)
    logs2 = score_logs(2, newest=True)
    gate_chunk(1, 0)()
    tails1, carries = suffix_sums(
        logs1, carries, after=[gate_chunk(1, 1), skip, pool_chunk(0), skip,
                               pool_chunk(1), skip])
    tails2, carries = suffix_sums(logs2, carries, newest=True)
    out_hooks = [pool_out_chunk(j // 2) if j % 2 == 0 else skip
                 for j in range(2 * wout_ref.shape[1] // MXU_COLS)]
    for c in range(len(chains)):
        weighted_values((0, 1, 2), [a0[c], weights(logs1, tails1, c),
                                    weights(logs2, tails2, c, newest=True)], c, first=True)
        run(out_hooks, c)

    def loop_body(state):
        n, _, carries = state
        logs = score_logs(n, without_newest=(n == 2))
        tails, carries = suffix_sums(logs, carries)
        for c in range(len(chains)):
            weighted_values((n,), [weights(logs, tails, c)], c, first=False)
        return n + 1, needed(n + 1, carries), carries

    def finish():
        att_t = jnp.concatenate(
            [jnp.concatenate([acc_s[p, s] for s in range(nsub)], axis=1)
             for p in range(N_PAIRS)], axis=0)
        y_sb = att_t.T * gate_s[1]
        y = y_s[...] + jnp.dot(y_sb.astype(BF16), wout_ref[POOL_WIDTH:, :],
                               preferred_element_type=F32)
        yn = y * lax.rsqrt(jnp.mean(y * y, axis=-1, keepdims=True) + EPS)
        o_ref[0, rows, :] = x_ref[0, rows, :] + yn * (gate * gpost_ref[...])

    finish()

    def further_steps():
        lax.while_loop(lambda state: state[1], loop_body, (jnp.int32(2), jnp.bool_(True), carries))
        finish()

    return needed(2, carries), further_steps


def _layer_kernel(x_ref, mod_ref, gpre_ref, win_ref, wpool_ref, pscale_ref, wout_ref,
                  gpost_ref, ucum_ref, o_ref, k_s, vt_s, qbd_s, acc_s, uprev_s, y_s, gate_s):
    step = pl.program_id(1)

    @pl.when(step == 0)
    def _():
        uprev_s[...] = jnp.zeros_like(uprev_s)
        k_s[0] = jnp.zeros_like(k_s[0])
        vt_s[0] = jnp.zeros_like(vt_s[0])

    unfinished = [
        _token_tile(step * TILES_PER_STEP + t, slice(t * TOKEN_TILE, (t + 1) * TOKEN_TILE),
                    x_ref, mod_ref, gpre_ref, win_ref, wpool_ref, pscale_ref, wout_ref,
                    gpost_ref, ucum_ref, o_ref, k_s, vt_s, qbd_s.at[t], acc_s.at[t], uprev_s,
                    y_s.at[t], gate_s.at[t])
        for t in range(TILES_PER_STEP)]
    for more_needed, further_steps in unfinished:
        pl.when(more_needed)(further_steps)


def _suffix_sum_matrix_t(k):
    later = np.arange(k)[None, :] > np.arange(k)[:, None]
    return jnp.asarray(np.concatenate([later, later], axis=1).astype(np.float32), dtype=BF16)


@jax.jit
def kernel(x, c, w_ada, b_ada, g_pre, w_in, w_pool, pool_scale, w_out, g_post):
    b, s, d = x.shape
    tq = TOKEN_TILE
    nsub = tq // ATT_TILE
    n_key_tiles = s // ATT_TILE
    step_rows = tq * TILES_PER_STEP
    assert s % step_rows == 0 and tq % ATT_TILE == 0 and tq >= MAX_WINDOW
    assert w_in.shape == (d, 2 * POOL_WIDTH + 4 * SB_WIDTH) and w_out.shape == (d, d)

    mod = pl.pallas_call(
        _ada_kernel,
        grid=(3,),
        in_specs=[pl.BlockSpec((b, d), lambda n: (0, 0)),
                  pl.BlockSpec((d, d), lambda n: (0, n)),
                  pl.BlockSpec((1, 1, d), lambda n: (n, 0, 0))],
        out_specs=pl.BlockSpec((1, b, d), lambda n: (n, 0, 0)),
        out_shape=jax.ShapeDtypeStruct((3, b, d), F32),
        name="adaln",
    )(c, w_ada, b_ada.reshape(3, 1, d))

    const = lambda *shape: pl.BlockSpec(shape, lambda bi, ti: (0,) * len(shape))
    return pl.pallas_call(
        _layer_kernel,
        grid=(b, s // step_rows),
        in_specs=[pl.BlockSpec((1, step_rows, d), lambda bi, ti: (bi, ti, 0)),
                  const(3, b, d),
                  const(1, d),
                  const(d, w_in.shape[1]),
                  const(len(POOL_WINDOWS), POOL_GROUP, POOL_GROUP),
                  const(1, POOL_WIDTH),
                  const(d, d),
                  const(1, d),
                  const(ATT_TILE, 2 * ATT_TILE)],
        out_specs=pl.BlockSpec((1, step_rows, d), lambda bi, ti: (bi, ti, 0)),
        out_shape=jax.ShapeDtypeStruct((b, s, d), F32),
        scratch_shapes=[
            pltpu.VMEM((n_key_tiles + 1, ATT_TILE, SB_WIDTH), BF16),
            pltpu.VMEM((n_key_tiles + 1, SB_WIDTH, ATT_TILE), BF16),
            pltpu.VMEM((TILES_PER_STEP, N_PAIRS, nsub, PAIR_WIDTH, 2 * ATT_TILE), BF16),
            pltpu.VMEM((TILES_PER_STEP, N_PAIRS, nsub, PAIR_WIDTH, ATT_TILE), F32),
            pltpu.VMEM((MAX_WINDOW, POOL_WIDTH), F32),
            pltpu.VMEM((TILES_PER_STEP, tq, d), F32),
            pltpu.VMEM((TILES_PER_STEP, 2, tq, POOL_WIDTH), F32)],
        compiler_params=pltpu.CompilerParams(
            dimension_semantics=("arbitrary", "arbitrary"),
            vmem_limit_bytes=VMEM_LIMIT_BYTES),
        name="layer",
    )(x, mod, g_pre.reshape(1, d), w_in.astype(BF16), w_pool.astype(BF16),
      pool_scale.reshape(1, POOL_WIDTH), w_out.astype(BF16), g_post.reshape(1, d),
      _suffix_sum_matrix_t(ATT_TILE))
```

```python
import jax
import jax.numpy as jnp
import numpy as np
from jax import lax
from jax.experimental import pallas as pl
from jax.experimental.pallas import tpu as pltpu

F32 = jnp.float32
BF16 = jnp.bfloat16

EPS = 1e-6
POOL_WINDOWS = (2, 4, 8, 16)
POOL_GROUP = 128
HEAD_DIM = 64
N_HEADS = 8
N_PAIRS = N_HEADS // 2
PAIR_WIDTH = 2 * HEAD_DIM
SB_WIDTH = N_HEADS * HEAD_DIM
POOL_WIDTH = len(POOL_WINDOWS) * POOL_GROUP
MAX_WINDOW = max(POOL_WINDOWS)
U_COL, POOL_GATE_COL = 0, POOL_WIDTH
Q_COL, K_COL, V_COL, SB_GATE_COL = (2 * POOL_WIDTH + j * SB_WIDTH for j in range(4))

TOKEN_TILE = 256
TILES_PER_STEP = 4
ATT_TILE = 128
EXTRA_KEYS = 16
MXU_COLS = 256
VMEM_LIMIT_BYTES = 56 * 1024 * 1024

SKIP_BELOW = -89.0
MASKED_SCORE = -1e30


def _sigmoid(x):
    return 1.0 / (1.0 + jnp.exp(-x))


def _ada_kernel(c_ref, w_ref, b_ref, o_ref):
    c = c_ref[...]
    o_ref[0] = jnp.dot(c * _sigmoid(c), w_ref[...], preferred_element_type=F32,
                       precision=lax.Precision.HIGHEST) + b_ref[0]


def _log_terms(z):
    neg_part = jnp.minimum(z, 0.0)
    neg_pos_part = neg_part - z
    log1p_term = jnp.log(1.0 + jnp.exp(neg_part + neg_pos_part))
    return neg_part - log1p_term, neg_pos_part - log1p_term


def _hi_lo(x):
    hi = x.astype(BF16)
    lo = (x - hi.astype(F32)).astype(BF16)
    return jnp.concatenate([hi, lo], axis=0)


def _token_tile(i, rows, x_ref, mod_ref, gpre_ref, win_ref, wpool_ref, pscale_ref, wout_ref,
                gpost_ref, ucum_ref, o_ref, k_s, vt_s, qbd_s, acc_s, uprev_s, y_s, gate_s):
    tq = rows.stop - rows.start
    nsub = tq // ATT_TILE

    x = x_ref[0, rows, :]
    batch = pl.ds(pl.program_id(0), 1)
    shift, scale, gate = mod_ref[0, batch, :], mod_ref[1, batch, :], mod_ref[2, batch, :]
    hn = x * lax.rsqrt(jnp.mean(x * x, axis=-1, keepdims=True) + EPS)
    hb = (hn * (gpre_ref[...] * (1.0 + scale)) + shift).astype(BF16)

    def proj(first_col, width):
        return jnp.dot(hb, win_ref[:, first_col:first_col + width], preferred_element_type=F32)


    qt = (proj(Q_COL, SB_WIDTH) * (1.0 / np.sqrt(HEAD_DIM))).T.astype(BF16)
    k = proj(K_COL, SB_WIDTH).astype(BF16)
    first_head = lax.broadcasted_iota(jnp.int32, (PAIR_WIDTH, ATT_TILE), 0) < HEAD_DIM
    zero = jnp.zeros((PAIR_WIDTH, ATT_TILE), BF16)
    for s in range(nsub):
        cols = slice(s * ATT_TILE, (s + 1) * ATT_TILE)
        k_s[i * nsub + s + 1] = k[cols, :]
        for p in range(N_PAIRS):
            qp = qt[p * PAIR_WIDTH:(p + 1) * PAIR_WIDTH, cols]
            qbd_s[p, s] = jnp.concatenate(
                [jnp.where(first_head, qp, zero), jnp.where(first_head, zero, qp)], axis=1)

    def values_chunk(half):
        def emit():
            rows = slice(half * MXU_COLS, (half + 1) * MXU_COLS)
            vt = proj(V_COL + half * MXU_COLS, MXU_COLS).T.astype(BF16)
            for s in range(nsub):
                vt_s[i * nsub + s + 1, rows, :] = vt[:, s * ATT_TILE:(s + 1) * ATT_TILE]
        return emit

    pooled_out = [None] * len(POOL_WINDOWS)
    u_half = [None] * (POOL_WIDTH // MXU_COLS)

    def pool_in_chunk(half):
        def emit():
            u_half[half] = proj(U_COL + half * MXU_COLS, MXU_COLS)
        return emit

    def pool_chunk(half):
        def emit():
            cols = slice(half * MXU_COLS, (half + 1) * MXU_COLS)
            u = u_half[half]
            uext = jnp.concatenate([uprev_s[:, cols], u], axis=0)
            uprev_s[:, cols] = u[tq - MAX_WINDOW:, :]
            pos = i * tq + lax.broadcasted_iota(jnp.int32, (tq, 1), 0)
            for gg in range(MXU_COLS // POOL_GROUP):
                g = half * (MXU_COLS // POOL_GROUP) + gg
                w = POOL_WINDOWS[g]
                ug = uext[:, gg * POOL_GROUP:(gg + 1) * POOL_GROUP]
                acc, width = ug, 1
                while width < w:
                    acc = acc + pltpu.roll(acc, width, axis=0)
                    width *= 2
                count = jnp.minimum(pos + 1, w).astype(F32)
                pooled = acc[MAX_WINDOW:, :] / count - ug[MAX_WINDOW:, :]
                pooled_out[g] = jnp.dot(pooled.astype(BF16), wpool_ref[g],
                                        preferred_element_type=F32)
        return emit

    def gate_chunk(which, half):
        def emit():
            cols = slice(half * MXU_COLS, (half + 1) * MXU_COLS)
            g = proj((POOL_GATE_COL, SB_GATE_COL)[which] + half * MXU_COLS, MXU_COLS)
            gate_s[which, :, cols] = g * _sigmoid(g)
        return emit

    gated = []

    def pool_out_chunk(j):
        def emit():
            if not gated:
                y_pool = jnp.concatenate(pooled_out, axis=-1) * pscale_ref[...]
                gated.append((y_pool * gate_s[0]).astype(BF16))
            cols = slice(j * MXU_COLS, (j + 1) * MXU_COLS)
            y_s[:, cols] = jnp.dot(gated[0], wout_ref[:POOL_WIDTH, cols],
                                   preferred_element_type=F32)
        return emit

    ucum_t = ucum_ref[...]
    key_idx = lax.broadcasted_iota(jnp.int32, (ATT_TILE, 2 * ATT_TILE), 0)
    query_idx = lax.broadcasted_iota(jnp.int32, (ATT_TILE, 2 * ATT_TILE), 1) % ATT_TILE
    causal = key_idx < query_idx
    last_row = lax.broadcasted_iota(jnp.int32, (8, 2 * ATT_TILE), 0) == 7
    extra_sums_t = jnp.where(
        lax.broadcasted_iota(jnp.int32, (EXTRA_KEYS, 2 * EXTRA_KEYS), 1) % EXTRA_KEYS
        > lax.broadcasted_iota(jnp.int32, (EXTRA_KEYS, 2 * EXTRA_KEYS), 0), 1.0, 0.0).astype(BF16)
    chains = [(s, p) for s in range(nsub) for p in range(N_PAIRS)]
    lanes = [slice(p * PAIR_WIDTH, (p + 1) * PAIR_WIDTH) for p in range(N_PAIRS)]

    def slots_of(n):
        return [jnp.maximum(i * nsub + s - n + 1, 0) for s in range(nsub)]

    def run(hooks, c):
        if c < len(hooks):
            hooks[c]()


    def score_logs(n, mask=None, newest=False, without_newest=None):
        keys = slice(ATT_TILE - EXTRA_KEYS, ATT_TILE) if newest else slice(0, ATT_TILE)
        slots = slots_of(n)
        z = [jnp.dot(k_s[slots[s], keys, lanes[p]], qbd_s[p, s], preferred_element_type=F32)
             for s, p in chains]
        logs = []
        for zc in z:
            if mask is not None:
                zc = jnp.where(mask, zc, MASKED_SCORE)
            if without_newest is not None:
                zc = jnp.concatenate(
                    [zc[:-EXTRA_KEYS],
                     jnp.where(without_newest, MASKED_SCORE, zc[-EXTRA_KEYS:])], axis=0)
            logs.append(_log_terms(zc))
        return logs

    def suffix_sums(logs, carries, newest=False, after=()):
        sums_t = extra_sums_t if newest else ucum_t
        tails, new_carries = [], []
        for c, (_, log_1mb) in enumerate(logs):
            if carries is None:
                sums = jnp.dot(sums_t, _hi_lo(log_1mb), preferred_element_type=F32)
            else:
                on_last = jnp.where(last_row, carries[c], 0.0)
                with_carry = jnp.concatenate([log_1mb[:-8], log_1mb[-8:] + on_last], axis=0)
                sums = jnp.dot(sums_t, _hi_lo(with_carry), preferred_element_type=F32)
                sums = jnp.concatenate([sums[:-8], sums[-8:] + on_last], axis=0)
            tails.append(sums)
            new_carries.append(sums[0:1, :] + log_1mb[0:1, :])
            run(after, c)
        return tails, tuple(new_carries)

    def weights(logs, tails, c, newest=False):
        a = jnp.exp(logs[c][0] + tails[c]).astype(BF16)
        if newest:
            a = jnp.concatenate(
                [jnp.zeros((ATT_TILE - EXTRA_KEYS, 2 * ATT_TILE), BF16), a], axis=0)
        return a

    def weighted_values(steps, a, c, first):
        s, p = chains[c]
        v_t = jnp.concatenate([vt_s[slots_of(n)[s], lanes[p], :] for n in steps], axis=1)
        o = jnp.dot(v_t, jnp.concatenate(a, axis=0), preferred_element_type=F32)
        o = jnp.concatenate([o[:HEAD_DIM, :ATT_TILE], o[HEAD_DIM:, ATT_TILE:]], axis=0)
        if first:
            acc_s[p, s] = o
        else:
            acc_s[p, s] += o

    def needed(n, carries):
        go = None
        for s in range(nsub):
            live = carries[s * N_PAIRS]
            for p in range(1, N_PAIRS):
                live = jnp.maximum(live, carries[s * N_PAIRS + p])
            go_s = jnp.logical_and(i * nsub + s - n >= 0, jnp.max(live) >= SKIP_BELOW)
            go = go_s if go is None else jnp.logical_or(go, go_s)
        return go

    skip = lambda: None
    pool_in_chunk(0)()
    logs0 = score_logs(0, mask=causal)
    values_chunk(0)()
    values_chunk(1)()
    tails0, carries = suffix_sums(logs0, None, after=[pool_in_chunk(1)])
    a0 = []
    gate_hooks = [skip, gate_chunk(0, 0), skip, gate_chunk(0, 1)]
    for c in range(len(chains)):
        a0.append(weights(logs0, tails0, c))
        run(gate_hooks, c)
    logs1 = score_logs(1)
    logs2 = score_logs(2, newest=True)
    gate_chunk(1, 0)()
    gate_chunk(1, 1)()
    tails1, carries = suffix_sums(
        logs1, carries, after=[skip, skip, pool_chunk(0), skip, pool_chunk(1), skip])
    tails2, carries = suffix_sums(logs2, carries, newest=True)
    out_hooks = [pool_out_chunk(j // 2) if j % 2 == 0 else skip
                 for j in range(2 * wout_ref.shape[1] // MXU_COLS)]
    for c in range(len(chains)):
        weighted_values((0, 1, 2), [a0[c], weights(logs1, tails1, c),
                                    weights(logs2, tails2, c, newest=True)], c, first=True)
        run(out_hooks, c)

    def loop_body(state):
        n, _, carries = state
        logs = score_logs(n, without_newest=(n == 2))
        tails, carries = suffix_sums(logs, carries)
        for c in range(len(chains)):
            weighted_values((n,), [weights(logs, tails, c)], c, first=False)
        return n + 1, needed(n + 1, carries), carries

    def finish():
        att_t = jnp.concatenate(
            [jnp.concatenate([acc_s[p, s] for s in range(nsub)], axis=1)
             for p in range(N_PAIRS)], axis=0)
        y_sb = att_t.T * gate_s[1]
        y = y_s[...] + jnp.dot(y_sb.astype(BF16), wout_ref[POOL_WIDTH:, :],
                               preferred_element_type=F32)
        yn = y * lax.rsqrt(jnp.mean(y * y, axis=-1, keepdims=True) + EPS)
        o_ref[0, rows, :] = x_ref[0, rows, :] + yn * (gate * gpost_ref[...])

    finish()

    def further_steps():
        lax.while_loop(lambda state: state[1], loop_body, (jnp.int32(2), jnp.bool_(True), carries))
        finish()

    return needed(2, carries), further_steps


def _layer_kernel(x_ref, mod_ref, gpre_ref, win_ref, wpool_ref, pscale_ref, wout_ref,
                  gpost_ref, ucum_ref, o_ref, k_s, vt_s, qbd_s, acc_s, uprev_s, y_s, gate_s):
    step = pl.program_id(1)

    @pl.when(step == 0)
    def _():
        uprev_s[...] = jnp.zeros_like(uprev_s)
        k_s[0] = jnp.zeros_like(k_s[0])
        vt_s[0] = jnp.zeros_like(vt_s[0])

    unfinished = [
        _token_tile(step * TILES_PER_STEP + t, slice(t * TOKEN_TILE, (t + 1) * TOKEN_TILE),
                    x_ref, mod_ref, gpre_ref, win_ref, wpool_ref, pscale_ref, wout_ref,
                    gpost_ref, ucum_ref, o_ref, k_s, vt_s, qbd_s.at[t], acc_s.at[t], uprev_s,
                    y_s.at[t], gate_s.at[t])
        for t in range(TILES_PER_STEP)]
    for more_needed, further_steps in unfinished:
        pl.when(more_needed)(further_steps)


def _suffix_sum_matrix_t(k):
    later = np.arange(k)[None, :] > np.arange(k)[:, None]
    return jnp.asarray(np.concatenate([later, later], axis=1).astype(np.float32), dtype=BF16)


@jax.jit
def kernel(x, c, w_ada, b_ada, g_pre, w_in, w_pool, pool_scale, w_out, g_post):
    b, s, d = x.shape
    tq = TOKEN_TILE
    nsub = tq // ATT_TILE
    n_key_tiles = s // ATT_TILE
    step_rows = tq * TILES_PER_STEP
    assert s % step_rows == 0 and tq % ATT_TILE == 0 and tq >= MAX_WINDOW
    assert w_in.shape == (d, 2 * POOL_WIDTH + 4 * SB_WIDTH) and w_out.shape == (d, d)

    mod = pl.pallas_call(
        _ada_kernel,
        grid=(3,),
        in_specs=[pl.BlockSpec((b, d), lambda n: (0, 0)),
                  pl.BlockSpec((d, d), lambda n: (0, n)),
                  pl.BlockSpec((1, 1, d), lambda n: (n, 0, 0))],
        out_specs=pl.BlockSpec((1, b, d), lambda n: (n, 0, 0)),
        out_shape=jax.ShapeDtypeStruct((3, b, d), F32),
        name="adaln",
    )(c, w_ada, b_ada.reshape(3, 1, d))

    const = lambda *shape: pl.BlockSpec(shape, lambda bi, ti: (0,) * len(shape))
    return pl.pallas_call(
        _layer_kernel,
        grid=(b, s // step_rows),
        in_specs=[pl.BlockSpec((1, step_rows, d), lambda bi, ti: (bi, ti, 0)),
                  const(3, b, d),
                  const(1, d),
                  const(d, w_in.shape[1]),
                  const(len(POOL_WINDOWS), POOL_GROUP, POOL_GROUP),
                  const(1, POOL_WIDTH),
                  const(d, d),
                  const(1, d),
                  const(ATT_TILE, 2 * ATT_TILE)],
        out_specs=pl.BlockSpec((1, step_rows, d), lambda bi, ti: (bi, ti, 0)),
        out_shape=jax.ShapeDtypeStruct((b, s, d), F32),
        scratch_shapes=[
            pltpu.VMEM((n_key_tiles + 1, ATT_TILE, SB_WIDTH), BF16),
            pltpu.VMEM((n_key_tiles + 1, SB_WIDTH, ATT_TILE), BF16),
            pltpu.VMEM((TILES_PER_STEP, N_PAIRS, nsub, PAIR_WIDTH, 2 * ATT_TILE), BF16),
            pltpu.VMEM((TILES_PER_STEP, N_PAIRS, nsub, PAIR_WIDTH, ATT_TILE), F32),
            pltpu.VMEM((MAX_WINDOW, POOL_WIDTH), F32),
            pltpu.VMEM((TILES_PER_STEP, tq, d), F32),
            pltpu.VMEM((TILES_PER_STEP, 2, tq, POOL_WIDTH), F32)],
        compiler_params=pltpu.CompilerParams(
            dimension_semantics=("arbitrary", "arbitrary"),
            vmem_limit_bytes=VMEM_LIMIT_BYTES),
        name="layer",
    )(x, mod, g_pre.reshape(1, d), w_in.astype(BF16), w_pool.astype(BF16),
      pool_scale.reshape(1, POOL_WIDTH), w_out.astype(BF16), g_post.reshape(1, d),
      _suffix_sum_matrix_t(ATT_TILE))
```
